```python
import math, functools
import jax, jax.numpy as jnp
from jax import lax
import numpy as np


D_MODEL = 1024
BATCH = 2
SEQ = 8192
DEPTH = 2
DEC_BATCH = 32
DEC_SEQ = 8
PAST_LEN = 16384
PAGE_SIZE = 128

D_RNN = D_MODEL
RNN_HEADS = 16
RNN_BLOCK = D_RNN // RNN_HEADS
CONV_W = 4
LRU_C = 8.0
ATT_HEADS = 8
ATT_DH = 64
ATT_VDIM = 2 * ATT_DH
ATT_WIDTH = ATT_HEADS * 2 * ATT_DH
N_IN = 2 * D_RNN + 3 * ATT_WIDTH + 2 * D_MODEL
N_BUCKETS = 32
MAX_EXACT = N_BUCKETS // 2
MAX_DIST = 128
Q_BLOCK = 128
D_FF = 2816
N_EXPERTS = 8
TOP_K = 2
E_FF = 2816
PLE_DIM = 256
N_DENSE = (DEPTH + 1) // 2
N_MOE = DEPTH // 2
ALPHA = (2.0 * DEPTH) ** 0.25
BETA = (8.0 * DEPTH) ** -0.25
LN_EPS = 1e-5
NEG_INF = -1e30

kernel_name = 'hybrid_rglru_diffattn_decoder_step'


def _layer_norm(x, g, b):
    x32 = x.astype(jnp.float32)
    mu = jnp.mean(x32, -1, keepdims=True)
    var = jnp.mean(jnp.square(x32 - mu), -1, keepdims=True)
    return ((x32 - mu) * lax.rsqrt(var + LN_EPS) * g.astype(jnp.float32) + b.astype(jnp.float32)).astype(x.dtype)


def _rel_bias(q_pos, k_pos, table):
    n = jnp.maximum(q_pos[:, None] - k_pos[None, :], 0)
    n_f = jnp.maximum(n, 1).astype(jnp.float32)
    large = MAX_EXACT + (jnp.log(n_f / MAX_EXACT) / math.log(MAX_DIST / MAX_EXACT)
                         * (N_BUCKETS - MAX_EXACT)).astype(jnp.int32)
    bucket = jnp.where(n < MAX_EXACT, n, jnp.minimum(large, N_BUCKETS - 1))
    return jnp.moveaxis(table[bucket].astype(jnp.float32), -1, 0)


def _diff_weights(s, bias, mask, lam):
    s = jnp.where(mask, s + bias[:, None], NEG_INF)
    p = jax.nn.softmax(s, axis=-1)
    return p[..., 0, :, :] - lam * p[..., 1, :, :]


def _prompt_attend(l, q, k, v, lam, rel_bias):
    bsz, seq = q.shape[0], q.shape[1]
    k_pos = jnp.arange(seq)

    def block(j):
        start = j * Q_BLOCK
        qb = lax.dynamic_slice_in_dim(q, start, Q_BLOCK, axis=1)
        q_pos = start + jnp.arange(Q_BLOCK)
        s = jnp.einsum('bqhcd,bkhcd->bhcqk', qb, k).astype(jnp.float32) * (ATT_DH ** -0.5)
        a = _diff_weights(s, _rel_bias(q_pos, k_pos, rel_bias), k_pos[None, :] <= q_pos[:, None], lam)
        return jnp.einsum('bhqk,bkhe->bqhe', a.astype(v.dtype), v)

    out = lax.map(block, jnp.arange(seq // Q_BLOCK))
    return jnp.moveaxis(out, 0, 1).reshape(bsz, seq, ATT_HEADS, ATT_VDIM)


def _sample_attend(l, q, k, v, lam, rel_bias, cache_k, cache_v, page_table):
    n_pages = PAST_LEN // PAGE_SIZE
    past = n_pages * PAGE_SIZE
    dec_seq = q.shape[1]
    q_pos = past + jnp.arange(dec_seq)
    k_pos = jnp.arange(past + dec_seq)
    bias = _rel_bias(q_pos, k_pos, rel_bias)
    mask = k_pos[None, :] <= q_pos[:, None]

    def one(args):
        qb, kb, vb, pages = args
        kp = cache_k[l, pages].reshape(past, ATT_HEADS, 2, ATT_DH)
        vp = cache_v[l, pages].reshape(past, ATT_HEADS, ATT_VDIM)
        s = jnp.concatenate([jnp.einsum('qhcd,khcd->hcqk', qb, kp),
                             jnp.einsum('qhcd,khcd->hcqk', qb, kb)], axis=-1)
        s = s.astype(jnp.float32) * (ATT_DH ** -0.5)
        a = _diff_weights(s, bias, mask, lam).astype(vb.dtype)
        return (jnp.einsum('hqk,khe->qhe', a[..., :past], vp)
                + jnp.einsum('hqk,khe->qhe', a[..., past:], vb))

    return lax.map(one, (q, k, v, page_table))


def _rg_lru(xc, h0, wr, br, wi, bi, lam_param):
    bsz, seq, _ = xc.shape
    xb = xc.reshape(bsz, seq, RNN_HEADS, RNN_BLOCK)
    r = jax.nn.sigmoid(jnp.einsum('bshi,hij->bshj', xb, wr).reshape(bsz, seq, D_RNN) + br)
    i = jax.nn.sigmoid(jnp.einsum('bshi,hij->bshj', xb, wi).reshape(bsz, seq, D_RNN) + bi)
    log_a = -LRU_C * r.astype(jnp.float32) * jax.nn.softplus(-lam_param.astype(jnp.float32))
    a = jnp.exp(log_a)
    b = jnp.sqrt(-jnp.expm1(2.0 * log_a)) * (i * xc).astype(jnp.float32)

    def step(h, ab):
        h = ab[0] * h + ab[1]
        return h, h

    h_last, hs = lax.scan(step, h0.astype(jnp.float32), (jnp.swapaxes(a, 0, 1), jnp.swapaxes(b, 0, 1)))
    return jnp.swapaxes(hs, 0, 1).astype(xc.dtype), h_last


def _mixer(x, l, conv_prev, h_prev, attend, w):
    bsz, seq, _ = x.shape
    z = x @ w['w_in'][l]
    cuts = [D_RNN, 2 * D_RNN, 2 * D_RNN + ATT_WIDTH, 2 * D_RNN + 2 * ATT_WIDTH,
            2 * D_RNN + 3 * ATT_WIDTH, 2 * D_RNN + 3 * ATT_WIDTH + D_MODEL]
    xr, gr, q, k, v, ga, gb = jnp.split(z, cuts, axis=-1)
    xpad = jnp.concatenate([conv_prev.astype(xr.dtype), xr], axis=1)
    cw = w['conv_w'][l]
    xc = w['conv_b'][l] + xpad[:, 0:seq] * cw[0]
    for t in range(1, CONV_W):
        xc = xc + xpad[:, t:t + seq] * cw[t]
    hs, h_last = _rg_lru(xc, h_prev, w['lru_wr'][l], w['lru_br'][l], w['lru_wi'][l],
                         w['lru_bi'][l], w['lru_lambda'][l])
    y_rnn = jax.nn.gelu(gr) * hs
    q = q.reshape(bsz, seq, ATT_HEADS, 2, ATT_DH)
    k = k.reshape(bsz, seq, ATT_HEADS, 2, ATT_DH)
    v = v.reshape(bsz, seq, ATT_HEADS, ATT_VDIM)
    lam_init = 0.8 - 0.6 * math.exp(-0.3 * l)
    f32 = jnp.float32
    lam = (jnp.exp(jnp.sum(w['lam_q1'][l].astype(f32) * w['lam_k1'][l].astype(f32)))
           - jnp.exp(jnp.sum(w['lam_q2'][l].astype(f32) * w['lam_k2'][l].astype(f32))) + lam_init)
    o = attend(l, q, k, v, lam).astype(f32)
    o = o * lax.rsqrt(jnp.mean(jnp.square(o), -1, keepdims=True) + LN_EPS) * w['subln_g'][l].astype(f32) * (1.0 - lam_init)
    y_att = o.reshape(bsz, seq, ATT_WIDTH).astype(x.dtype)
    merged = (jax.nn.sigmoid(ga) * (y_rnn @ w['w_rnn_out'][l])
              + jax.nn.sigmoid(gb) * (y_att @ w['w_attn_out'][l]))
    return (merged @ w['w_o'][l], k.reshape(bsz, seq, ATT_HEADS, 2 * ATT_DH), v,
            h_last, xpad[:, -(CONV_W - 1):])


def _swiglu(x, w1, w3, w2):
    return (jax.nn.silu(x @ w1) * (x @ w3)) @ w2


def _moe(x, router, w1, w3, w2):
    logits = (x @ router).astype(jnp.float32)
    vals, idx = lax.top_k(logits, TOP_K)
    gates = jax.nn.softmax(vals, axis=-1)
    combine = jnp.sum(jax.nn.one_hot(idx, N_EXPERTS, dtype=jnp.float32) * gates[..., None], axis=-2)
    y = jnp.zeros_like(x)
    for e in range(N_EXPERTS):
        y = y + combine[..., e:e + 1].astype(x.dtype) * _swiglu(x, w1[e], w3[e], w2[e])
    return y


def _trunk(x, p, conv_state, h_state, attend, w):
    new_k, new_v, new_h, new_conv = [], [], [], []
    for l in range(DEPTH):
        mix, k, v, h_last, conv_last = _mixer(x, l, conv_state[l], h_state[l], attend, w)
        x = _layer_norm(ALPHA * x + mix, w['ln1_g'][l], w['ln1_b'][l])
        m = l // 2
        if l % 2 == 0:
            ffn = _swiglu(x, w['ffn_w1'][m], w['ffn_w3'][m], w['ffn_w2'][m])
        else:
            ffn = _moe(x, w['router_w'][m], w['moe_w1'][m], w['moe_w3'][m], w['moe_w2'][m])
        ple = jax.nn.sigmoid(x @ w['ple_gate_w'][l]) * (p[l] @ w['ple_proj_w'][l])
        x = _layer_norm(ALPHA * x + ffn + ple, w['ln2_g'][l], w['ln2_b'][l])
        new_k.append(k)
        new_v.append(v)
        new_h.append(h_last)
        new_conv.append(conv_last)
    return x, jnp.stack(new_k), jnp.stack(new_v), jnp.stack(new_h), jnp.stack(new_conv)


def setup_inputs(seed: int = 0) -> dict:
    key = jax.random.key(seed)
    ks = iter(jax.random.split(key, 48))

    def nrm(shape, scale):
        return scale * jax.random.normal(next(ks), shape, jnp.float32)

    n_pages = PAST_LEN // PAGE_SIZE
    n_used = DEC_BATCH * n_pages
    n_pool = n_used + max(1, n_used // 4)
    page_table = jax.random.permutation(next(ks), n_pool)[:n_used].reshape(DEC_BATCH, n_pages).astype(jnp.int32)
    u = jax.random.uniform(next(ks), (DEPTH, D_RNN), jnp.float32, 0.9, 0.999)
    s = u ** (1.0 / LRU_C)
    lru_lambda = jnp.log(s) - jnp.log1p(-s)
    return {
        'x_prompt': nrm((BATCH, SEQ, D_MODEL), 1.0),
        'x_sample': nrm((DEC_BATCH, DEC_SEQ, D_MODEL), 1.0),
        'cache_k': nrm((DEPTH, n_pool, PAGE_SIZE, ATT_HEADS, 2 * ATT_DH), 1.0),
        'cache_v': nrm((DEPTH, n_pool, PAGE_SIZE, ATT_HEADS, ATT_VDIM), 1.0),
        'state_h': nrm((DEPTH, DEC_BATCH, D_RNN), 0.5),
        'state_conv': nrm((DEPTH, DEC_BATCH, CONV_W - 1, D_RNN), 1.0),
        'page_table': page_table,
        'p_prompt': nrm((DEPTH, BATCH, SEQ, PLE_DIM), 1.0),
        'p_sample': nrm((DEPTH, DEC_BATCH, DEC_SEQ, PLE_DIM), 1.0),
        'rel_bias': nrm((N_BUCKETS, ATT_HEADS), 0.5),
        'w_in': nrm((DEPTH, D_MODEL, N_IN), D_MODEL ** -0.5),
        'conv_w': nrm((DEPTH, CONV_W, D_RNN), CONV_W ** -0.5),
        'conv_b': nrm((DEPTH, D_RNN), 0.01),
        'lru_wr': nrm((DEPTH, RNN_HEADS, RNN_BLOCK, RNN_BLOCK), RNN_BLOCK ** -0.5),
        'lru_br': nrm((DEPTH, D_RNN), 0.01),
        'lru_wi': nrm((DEPTH, RNN_HEADS, RNN_BLOCK, RNN_BLOCK), RNN_BLOCK ** -0.5),
        'lru_bi': nrm((DEPTH, D_RNN), 0.01),
        'lru_lambda': lru_lambda,
        'w_rnn_out': nrm((DEPTH, D_RNN, D_MODEL), D_RNN ** -0.5),
        'w_attn_out': nrm((DEPTH, ATT_WIDTH, D_MODEL), ATT_WIDTH ** -0.5),
        'lam_q1': nrm((DEPTH, ATT_DH), 0.1),
        'lam_k1': nrm((DEPTH, ATT_DH), 0.1),
        'lam_q2': nrm((DEPTH, ATT_DH), 0.1),
        'lam_k2': nrm((DEPTH, ATT_DH), 0.1),
        'subln_g': 1.0 + nrm((DEPTH, ATT_VDIM), 0.01),
        'w_o': nrm((DEPTH, D_MODEL, D_MODEL), BETA * D_MODEL ** -0.5),
        'ln1_g': 1.0 + nrm((DEPTH, D_MODEL), 0.01),
        'ln1_b': nrm((DEPTH, D_MODEL), 0.01),
        'ffn_w1': nrm((N_DENSE, D_MODEL, D_FF), D_MODEL ** -0.5),
        'ffn_w3': nrm((N_DENSE, D_MODEL, D_FF), D_MODEL ** -0.5),
        'ffn_w2': nrm((N_DENSE, D_FF, D_MODEL), BETA * D_FF ** -0.5),
        'router_w': nrm((N_MOE, D_MODEL, N_EXPERTS), D_MODEL ** -0.5),
        'moe_w1': nrm((N_MOE, N_EXPERTS, D_MODEL, E_FF), D_MODEL ** -0.5),
        'moe_w3': nrm((N_MOE, N_EXPERTS, D_MODEL, E_FF), D_MODEL ** -0.5),
        'moe_w2': nrm((N_MOE, N_EXPERTS, E_FF, D_MODEL), BETA * E_FF ** -0.5),
        'ple_gate_w': nrm((DEPTH, D_MODEL, D_MODEL), D_MODEL ** -0.5),
        'ple_proj_w': nrm((DEPTH, PLE_DIM, D_MODEL), BETA * PLE_DIM ** -0.5),
        'ln2_g': 1.0 + nrm((DEPTH, D_MODEL), 0.01),
        'ln2_b': nrm((DEPTH, D_MODEL), 0.01),
    }


def reference(x_prompt, x_sample, cache_k, cache_v, state_h, state_conv, page_table,
              p_prompt, p_sample, rel_bias, w_in, conv_w, conv_b, lru_wr, lru_br,
              lru_wi, lru_bi, lru_lambda, w_rnn_out, w_attn_out, lam_q1, lam_k1,
              lam_q2, lam_k2, subln_g, w_o, ln1_g, ln1_b, ffn_w1, ffn_w3, ffn_w2,
              router_w, moe_w1, moe_w3, moe_w2, ple_gate_w, ple_proj_w, ln2_g, ln2_b):
    w = dict(w_in=w_in, conv_w=conv_w, conv_b=conv_b, lru_wr=lru_wr, lru_br=lru_br,
             lru_wi=lru_wi, lru_bi=lru_bi, lru_lambda=lru_lambda, w_rnn_out=w_rnn_out,
             w_attn_out=w_attn_out, lam_q1=lam_q1, lam_k1=lam_k1, lam_q2=lam_q2,
             lam_k2=lam_k2, subln_g=subln_g, w_o=w_o, ln1_g=ln1_g, ln1_b=ln1_b,
             ffn_w1=ffn_w1, ffn_w3=ffn_w3, ffn_w2=ffn_w2, router_w=router_w,
             moe_w1=moe_w1, moe_w3=moe_w3, moe_w2=moe_w2, ple_gate_w=ple_gate_w,
             ple_proj_w=ple_proj_w, ln2_g=ln2_g, ln2_b=ln2_b)
    bsz = x_prompt.shape[0]
    conv0 = jnp.zeros((DEPTH, bsz, CONV_W - 1, D_RNN), x_prompt.dtype)
    h0 = jnp.zeros((DEPTH, bsz, D_RNN), jnp.float32)
    y_prompt, k_prompt, v_prompt, h_prompt, conv_prompt = _trunk(
        x_prompt, p_prompt, conv0, h0, functools.partial(_prompt_attend, rel_bias=rel_bias), w)
    y_sample, k_sample, v_sample, h_sample, conv_sample = _trunk(
        x_sample, p_sample, state_conv, state_h,
        functools.partial(_sample_attend, rel_bias=rel_bias, cache_k=cache_k,
                          cache_v=cache_v, page_table=page_table), w)
    return (y_prompt, y_sample, k_prompt, v_prompt, h_prompt, conv_prompt,
            k_sample, v_sample, h_sample, conv_sample)
```

```python
import functools
import math

import jax
import jax.numpy as jnp
from jax import lax
from jax.experimental import pallas as pl
from jax.experimental.pallas import tpu as pltpu

F32 = jnp.float32
BF16 = jnp.bfloat16

D_MODEL = 1024
D_RNN = 1024
RNN_HEADS = 16
RNN_BLOCK = D_RNN // RNN_HEADS
CONV_W = 4
LRU_C = 8.0
ATT_HEADS = 8
ATT_DH = 64
ATT_VDIM = 2 * ATT_DH
N_PLANES = 7
N_BUCKETS = 32
MAX_EXACT = N_BUCKETS // 2
MAX_DIST = 128
N_EXPERTS = 8
PLE_DIM = 256
LN_EPS = 1e-5
NEG_INF = -1e30
QK_SCALE = ATT_DH ** -0.5

LANES = 128
SUBLANES = 8
MXU_DIM = 256
VMEM_LIMIT = 56 << 20


def _cparams(sem):
    return pltpu.CompilerParams(dimension_semantics=sem, vmem_limit_bytes=VMEM_LIMIT)


def _sds(shape, dtype):
    return jax.ShapeDtypeStruct(shape, dtype)


def _layer_norm_rows(y, g, b):
    mu = jnp.mean(y, axis=-1, keepdims=True)
    yc = y - mu
    var = jnp.mean(yc * yc, axis=-1, keepdims=True)
    return yc * lax.rsqrt(var + LN_EPS) * g + b


def _inproj_body(x_ref, w_ref, z_ref, zb_ref, xb_scr):
    j = pl.program_id(1)

    @pl.when(j == 0)
    def _():
        xb_scr[...] = x_ref[...].astype(BF16)

    acc = jnp.dot(xb_scr[...], w_ref[...], preferred_element_type=F32)
    z_ref[...] = acc

    @pl.when(j == 2)
    def _():
        zb_ref[...] = (acc * QK_SCALE).astype(BF16)

    @pl.when(jnp.logical_or(j == 3, j == 4))
    def _():
        zb_ref[...] = acc.astype(BF16)


def _inproj(x, w_b, tm):
    m = x.shape[0]
    return pl.pallas_call(
        _inproj_body,
        grid=(m // tm, N_PLANES),
        in_specs=[pl.BlockSpec((tm, D_MODEL), lambda i, j: (i, 0)),
                  pl.BlockSpec((D_MODEL, D_MODEL), lambda i, j: (0, j))],
        out_specs=[pl.BlockSpec((None, tm, D_MODEL), lambda i, j: (j, i, 0)),
                   pl.BlockSpec((None, tm, D_MODEL), lambda i, j: (jnp.clip(j - 2, 0, 2), i, 0))],
        out_shape=[_sds((N_PLANES, m, D_MODEL), F32), _sds((3, m, D_MODEL), BF16)],
        scratch_shapes=[pltpu.VMEM((tm, D_MODEL), BF16)],
        compiler_params=_cparams(("arbitrary", "arbitrary")),
        name="inproj",
    )(x, w_b)


def _rnn_body(xr_ref, gr_ref, conv0_ref, h0_ref, cw_ref, cb_ref, wr_ref, br_ref, wi_ref, bi_ref,
              lam_ref, y_ref, hlast_ref, xpad_scr, a_scr, b_scr, h_scr, hs_scr):
    t = pl.program_id(1)
    tt = xr_ref.shape[0]
    halo = SUBLANES

    @pl.when(t == 0)
    def _():
        xpad_scr[halo - 3:halo, :] = conv0_ref[...]
        h_scr[...] = h0_ref[...]

    xr = xr_ref[...]
    xpad_scr[halo:halo + tt, :] = xr
    cw = cw_ref[...]
    xc = cb_ref[...] + xpad_scr[halo - 3:halo - 3 + tt, :] * cw[0:1]
    xc = xc + xpad_scr[halo - 2:halo - 2 + tt, :] * cw[1:2]
    xc = xc + xpad_scr[halo - 1:halo - 1 + tt, :] * cw[2:3]
    xc = xc + xr * cw[3:4]
    tail = xpad_scr[halo + tt - 3:halo + tt, :]
    xpad_scr[halo - 3:halo, :] = tail

    xcb = xc.astype(BF16)
    n_grp = D_RNN // MXU_DIM
    r_parts, i_parts = [], []
    for g in range(n_grp):
        xg = xcb[:, g * MXU_DIM:(g + 1) * MXU_DIM]
        r_parts.append(jnp.dot(xg, wr_ref[g], preferred_element_type=F32))
        i_parts.append(jnp.dot(xg, wi_ref[g], preferred_element_type=F32))
    r = jax.nn.sigmoid(jnp.concatenate(r_parts, axis=1) + br_ref[...])
    ig = jax.nn.sigmoid(jnp.concatenate(i_parts, axis=1) + bi_ref[...])
    log_a = -LRU_C * r * jax.nn.softplus(-lam_ref[...])
    a = jnp.exp(log_a)
    a_scr[...] = a
    b_scr[...] = jnp.sqrt(-jnp.tanh(log_a) * (a * a + 1.0)) * (ig * xc)

    def step(i, h):
        h = a_scr[pl.ds(i, 1), :] * h + b_scr[pl.ds(i, 1), :]
        hs_scr[pl.ds(i, 1), :] = h
        return h

    h = lax.fori_loop(0, tt, step, h_scr[...], unroll=8)
    h_scr[...] = h
    y_ref[...] = (jax.nn.gelu(gr_ref[...]) * hs_scr[...]).astype(BF16)

    @pl.when(t == pl.num_programs(1) - 1)
    def _():
        hlast_ref[...] = h


def _rnn(z, conv0, h0, cw, cb, wr_b, br, wi_b, bi, lam, bsz, seq, tt):
    m = bsz * seq
    nt = seq // tt
    n_grp = D_RNN // MXU_DIM
    row = lambda b, t: (0, 0)
    return pl.pallas_call(
        _rnn_body,
        grid=(bsz, nt),
        in_specs=[pl.BlockSpec((None, tt, D_RNN), lambda b, t: (0, b * nt + t, 0)),
                  pl.BlockSpec((None, tt, D_RNN), lambda b, t: (1, b * nt + t, 0)),
                  pl.BlockSpec((None, CONV_W - 1, D_RNN), lambda b, t: (b, 0, 0)),
                  pl.BlockSpec((None, 1, D_RNN), lambda b, t: (b, 0, 0)),
                  pl.BlockSpec((CONV_W, D_RNN), row),
                  pl.BlockSpec((1, D_RNN), row),
                  pl.BlockSpec((n_grp, MXU_DIM, MXU_DIM), lambda b, t: (0, 0, 0)),
                  pl.BlockSpec((1, D_RNN), row),
                  pl.BlockSpec((n_grp, MXU_DIM, MXU_DIM), lambda b, t: (0, 0, 0)),
                  pl.BlockSpec((1, D_RNN), row),
                  pl.BlockSpec((1, D_RNN), row)],
        out_specs=[pl.BlockSpec((tt, D_RNN), lambda b, t: (b * nt + t, 0)),
                   pl.BlockSpec((None, 1, D_RNN), lambda b, t: (b, 0, 0))],
        out_shape=[_sds((m, D_RNN), BF16), _sds((bsz, 1, D_RNN), F32)],
        scratch_shapes=[pltpu.VMEM((tt + SUBLANES, D_RNN), F32),
                        pltpu.VMEM((tt, D_RNN), F32),
                        pltpu.VMEM((tt, D_RNN), F32),
                        pltpu.VMEM((1, D_RNN), F32),
                        pltpu.VMEM((tt, D_RNN), F32)],
        compiler_params=_cparams(("arbitrary", "arbitrary")),
        name="rnn",
    )(z, z, conv0, h0, cw, cb, wr_b, br, wi_b, bi, lam)


def _block_diag_groups(w):
    per = MXU_DIM // RNN_BLOCK
    n_grp = RNN_HEADS // per
    wg = w.reshape(n_grp, per, RNN_BLOCK, RNN_BLOCK)
    eye = jnp.eye(per, dtype=w.dtype)
    out = jnp.einsum('gpij,pq->gpiqj', wg, eye)
    return out.reshape(n_grp, MXU_DIM, MXU_DIM)


def _bucket(n):
    n_f = jnp.maximum(n, 1).astype(F32)
    large = MAX_EXACT + (jnp.log(n_f / MAX_EXACT) / math.log(MAX_DIST / MAX_EXACT)
                         * (N_BUCKETS - MAX_EXACT)).astype(jnp.int32)
    return jnp.where(n < MAX_EXACT, n, jnp.minimum(large, N_BUCKETS - 1))


def _lambda_value(lamp_ref, lam_init):
    lp = lamp_ref[...]
    s1 = jnp.sum(lp[0:1] * lp[1:2], axis=-1, keepdims=True)
    s2 = jnp.sum(lp[2:3] * lp[3:4], axis=-1, keepdims=True)
    return jnp.exp(s1) - jnp.exp(s2) + lam_init


def _prompt_bias(rel_bias, tq):
    i = jnp.arange(tq)[:, None]
    j = jnp.arange(tq)[None, :]
    tab = rel_bias.astype(F32)
    far = jnp.broadcast_to(tab[N_BUCKETS - 1][:, None, None], (ATT_HEADS, tq, tq))
    prev = jnp.moveaxis(tab[_bucket(tq + i - j)], -1, 0)
    diag = jnp.where((j <= i)[None], jnp.moveaxis(tab[_bucket(jnp.maximum(i - j, 0))], -1, 0), NEG_INF)
    b3 = jnp.stack([far, prev, diag], axis=1)
    return jnp.concatenate([b3, b3], axis=2)


def _pattn_body(q_ref, k_ref, v_ref, bias_ref, lamp_ref, g_ref, o_ref, *, lam_init):
    qi = pl.program_id(2)
    tq = q_ref.shape[0]
    q = q_ref[...]
    lane = lax.broadcasted_iota(jnp.int32, q.shape, 1)
    zero = jnp.zeros_like(q)
    qq = jnp.concatenate([jnp.where(lane < ATT_DH, q, zero), jnp.where(lane >= ATT_DH, q, zero)], axis=0)

    def step(ki, carry):
        m, l, acc = carry
        start = pl.multiple_of(ki * tq, tq)
        kb = k_ref[pl.ds(start, tq), :]
        vb = v_ref[pl.ds(start, tq), :]
        sel = jnp.clip(ki - (qi - 2), 0, 2)
        s = lax.dot_general(qq, kb, (((1,), (1,)), ((), ())), preferred_element_type=F32)
        s = s + bias_ref[sel]
        m_new = jnp.maximum(m, jnp.max(s, axis=1, keepdims=True))
        alpha = jnp.exp(m - m_new)
        p = jnp.exp(s - m_new)
        l = alpha * l + jnp.sum(p, axis=1, keepdims=True)
        acc = alpha * acc + jnp.dot(p.astype(BF16), vb, preferred_element_type=F32)
        return m_new, l, acc

    init = (jnp.full((2 * tq, 1), NEG_INF, F32), jnp.zeros((2 * tq, 1), F32),
            jnp.zeros((2 * tq, ATT_VDIM), F32))
    _, l, acc = lax.fori_loop(0, qi + 1, step, init)
    a = acc / l
    lam = _lambda_value(lamp_ref, lam_init)
    o = a[:tq] - lam * a[tq:]
    o = o * lax.rsqrt(jnp.mean(o * o, axis=-1, keepdims=True) + LN_EPS) * g_ref[...] * (1.0 - lam_init)
    o_ref[...] = o.astype(BF16)


def _prompt_attend(zb, bias, lamp, g, bsz, seq, tq, lam_init):
    m = bsz * seq
    nq = seq // tq
    return pl.pallas_call(
        functools.partial(_pattn_body, lam_init=lam_init),
        grid=(bsz, ATT_HEADS, nq),
        in_specs=[pl.BlockSpec((None, tq, ATT_VDIM), lambda b, h, i: (0, b * nq + i, h)),
                  pl.BlockSpec((None, seq, ATT_VDIM), lambda b, h, i: (1, b, h)),
                  pl.BlockSpec((None, seq, ATT_VDIM), lambda b, h, i: (2, b, h)),
                  pl.BlockSpec((None, 3, 2 * tq, tq), lambda b, h, i: (h, 0, 0, 0)),
                  pl.BlockSpec((4, ATT_DH), lambda b, h, i: (0, 0)),
                  pl.BlockSpec((1, ATT_VDIM), lambda b, h, i: (0, 0))],
        out_specs=pl.BlockSpec((tq, ATT_VDIM), lambda b, h, i: (b * nq + i, h)),
        out_shape=_sds((m, ATT_HEADS * ATT_VDIM), BF16),
        compiler_params=_cparams(("arbitrary", "arbitrary", "arbitrary")),
        name="prompt_attn",
    )(zb, zb, zb, bias, lamp, g)


def _sattn_body(pt_ref, *refs, n_pp, lam_init):
    del pt_ref
    kp_refs = refs[:n_pp]
    vp_refs = refs[n_pp:2 * n_pp]
    (wq_ref, knew_ref, vnew_ref, bfar_ref, blast_ref, bnew_ref, mask_ref, lamp_ref, g_ref,
     o_ref, m_scr, l_scr, acc_scr) = refs[2 * n_pp:]
    c = pl.program_id(1)
    last = c == pl.num_programs(1) - 1
    rows = kp_refs[0].shape[0]
    n_tok = rows // ATT_HEADS

    @pl.when(c == 0)
    def _():
        m_scr[...] = jnp.full(m_scr.shape, NEG_INF, F32)
        l_scr[...] = jnp.zeros(l_scr.shape, F32)
        acc_scr[...] = jnp.zeros(acc_scr.shape, F32)

    wq = wq_ref[...]
    valid = mask_ref[...] > 0.0

    def update(s_list, v_list):
        mx = jnp.max(s_list[0], axis=0)
        for s3 in s_list[1:]:
            mx = jnp.maximum(mx, jnp.max(s3, axis=0))
        m_old = m_scr[...]
        m_new = jnp.maximum(m_old, mx)
        alpha = jnp.exp(m_old - m_new)
        l = alpha * l_scr[...]
        pv = jnp.zeros(acc_scr.shape, F32)
        for s3, vv in zip(s_list, v_list):
            p = jnp.where(valid[None], jnp.exp(s3 - m_new[None]), 0.0)
            l = l + jnp.sum(p, axis=0)
            p2 = p.reshape(vv.shape[0], LANES).astype(BF16)
            pv = pv + lax.dot_general(vv, p2, (((0,), (0,)), ((), ())), preferred_element_type=F32)
        alpha_row = jnp.sum(jnp.where(valid, alpha, 0.0), axis=0, keepdims=True)
        acc_scr[...] = acc_scr[...] * alpha_row + pv
        m_scr[...] = m_new
        l_scr[...] = l

    s_list, v_list = [], []
    for p in range(n_pp):
        kp = kp_refs[p][...].astype(BF16)
        s3 = jnp.dot(kp, wq, preferred_element_type=F32).reshape(n_tok, ATT_HEADS, LANES)
        if p == n_pp - 1:
            bias = jnp.where(last, blast_ref[...].reshape(n_tok, ATT_HEADS, LANES), bfar_ref[...][None])
        else:
            bias = bfar_ref[...][None]
        s_list.append(s3 + bias)
        v_list.append(vp_refs[p][...].astype(BF16))
    update(s_list, v_list)

    @pl.when(last)
    def _():
        n_new = knew_ref.shape[0] // ATT_HEADS
        sn = jnp.dot(knew_ref[...].astype(BF16), wq, preferred_element_type=F32) + bnew_ref[...]
        update([sn.reshape(n_new, ATT_HEADS, LANES)], [vnew_ref[...].astype(BF16)])
        l_row = jnp.sum(jnp.where(valid, l_scr[...], 0.0), axis=0, keepdims=True)
        a = acc_scr[...] / l_row
        lam = _lambda_value(lamp_ref, lam_init)
        half = LANES // 2
        o = a[:, :half] - lam * a[:, half:]
        o = o * lax.rsqrt(jnp.mean(o * o, axis=0, keepdims=True) + LN_EPS) * g_ref[...] * (1.0 - lam_init)
        o_ref[...] = o


def _sample_attend(layer, page_table, cache_k4, cache_v4, wq, knew, vnew, bfar, blast, bnew, mask,
                   lamp, g_col, n_pp, lam_init):
    bsz, n_pages = page_table.shape
    rows = cache_k4.shape[2]
    n_new = knew.shape[1]
    nc = n_pages // n_pp

    def page_spec(p):
        return pl.BlockSpec((None, None, rows, LANES),
                            lambda b, c, pt: (layer, pt[b, c * n_pp + p], 0, 0))

    const2 = lambda b, c, pt: (0, 0)
    in_specs = ([page_spec(p) for p in range(n_pp)] + [page_spec(p) for p in range(n_pp)] + [
        pl.BlockSpec((None, LANES, LANES), lambda b, c, pt: (b, 0, 0)),
        pl.BlockSpec((None, n_new, LANES), lambda b, c, pt: (b, 0, 0)),
        pl.BlockSpec((None, n_new, LANES), lambda b, c, pt: (b, 0, 0)),
        pl.BlockSpec((SUBLANES, LANES), const2),
        pl.BlockSpec((rows, LANES), const2),
        pl.BlockSpec((n_new, LANES), const2),
        pl.BlockSpec((SUBLANES, LANES), const2),
        pl.BlockSpec((4, ATT_DH), const2),
        pl.BlockSpec((ATT_VDIM, 1), const2)])
    grid_spec = pltpu.PrefetchScalarGridSpec(
        num_scalar_prefetch=1,
        grid=(bsz, nc),
        in_specs=in_specs,
        out_specs=pl.BlockSpec((None, ATT_VDIM, LANES // 2), lambda b, c, pt: (b, 0, 0)),
        scratch_shapes=[pltpu.VMEM((SUBLANES, LANES), F32),
                        pltpu.VMEM((SUBLANES, LANES), F32),
                        pltpu.VMEM((ATT_VDIM, LANES), F32)])
    return pl.pallas_call(
        functools.partial(_sattn_body, n_pp=n_pp, lam_init=lam_init),
        grid_spec=grid_spec,
        out_shape=_sds((bsz, ATT_VDIM, LANES // 2), F32),
        compiler_params=_cparams(("arbitrary", "arbitrary")),
        name="sample_attn",
    )(page_table, *([cache_k4] * n_pp), *([cache_v4] * n_pp), wq, knew, vnew, bfar, blast, bnew, mask,
      lamp, g_col)


def _sample_tables(rel_bias, page_size, dec_seq):
    tab = rel_bias.astype(F32)
    col = jnp.arange(LANES)
    col_h = (col % (LANES // 2)) // dec_seq
    col_q = col % dec_seq
    row_h = jnp.arange(ATT_HEADS)
    mask = (row_h[:, None] == col_h[None, :]).astype(F32)
    bfar = jnp.broadcast_to(tab[N_BUCKETS - 1][:, None], (ATT_HEADS, LANES))
    t = jnp.arange(page_size)
    dist = page_size + col_q[None, :] - t[:, None]
    blast = tab[_bucket(dist)]
    blast = jnp.moveaxis(blast, -1, 1).reshape(page_size * ATT_HEADS, LANES)
    j = jnp.arange(dec_seq)
    dn = col_q[None, :] - j[:, None]
    bnew = jnp.where((dn >= 0)[:, None, :], jnp.moveaxis(tab[_bucket(jnp.maximum(dn, 0))], -1, 1), NEG_INF)
    bnew = bnew.reshape(dec_seq * ATT_HEADS, LANES)
    return mask, bfar, blast, bnew


def _sample_wq(q_plane, bsz, dec_seq):
    q = q_plane.reshape(bsz, dec_seq, ATT_HEADS, 2, ATT_DH) * QK_SCALE
    eye = jnp.eye(2, dtype=q.dtype)
    w = jnp.einsum('bqhcd,ce->bcdehq', q, eye)
    return w.reshape(bsz, 2 * ATT_DH, 2 * ATT_HEADS * dec_seq).astype(BF16)


def _merge_body(yr_ref, ya_ref, ga_ref, gb_ref, x_ref, wr_ref, wa_ref, wo_ref, g_ref, b_ref, o_ref,
                *, alpha):
    pr = jnp.dot(yr_ref[...], wr_ref[...], preferred_element_type=F32)
    pa = jnp.dot(ya_ref[...], wa_ref[...], preferred_element_type=F32)
    merged = jax.nn.sigmoid(ga_ref[...]) * pr + jax.nn.sigmoid(gb_ref[...]) * pa
    mix = jnp.dot(merged.astype(BF16), wo_ref[...], preferred_element_type=F32)
    o_ref[...] = _layer_norm_rows(alpha * x_ref[...] + mix, g_ref[...], b_ref[...])


def _merge(yr, ya, z, x, wr_b, wa_b, wo_b, g, b, tm, alpha):
    m = x.shape[0]
    rows = pl.BlockSpec((tm, D_MODEL), lambda i: (i, 0))
    full = pl.BlockSpec((D_MODEL, D_MODEL), lambda i: (0, 0))
    vec = pl.BlockSpec((1, D_MODEL), lambda i: (0, 0))
    return pl.pallas_call(
        functools.partial(_merge_body, alpha=alpha),
        grid=(m // tm,),
        in_specs=[rows, rows,
                  pl.BlockSpec((None, tm, D_MODEL), lambda i: (5, i, 0)),
                  pl.BlockSpec((None, tm, D_MODEL), lambda i: (6, i, 0)),
                  rows, full, full, full, vec, vec],
        out_specs=rows,
        out_shape=_sds((m, D_MODEL), F32),
        compiler_params=_cparams(("arbitrary",)),
        name="merge",
    )(yr, ya, z, z, x, wr_b, wa_b, wo_b, g, b)


def _top2_combine(x, rw_ref):
    logits = jnp.dot(x, rw_ref[...], preferred_element_type=F32, precision=lax.Precision.HIGHEST)
    lane = lax.broadcasted_iota(jnp.int32, logits.shape, 1).astype(F32)
    big = float(LANES)
    lg = jnp.where(lane < N_EXPERTS, logits, -jnp.inf)
    m1 = jnp.max(lg, axis=1, keepdims=True)
    i1 = jnp.min(jnp.where(lg == m1, lane, big), axis=1, keepdims=True)
    lg2 = jnp.where(lane == i1, -jnp.inf, lg)
    m2 = jnp.max(lg2, axis=1, keepdims=True)
    i2 = jnp.min(jnp.where(lg2 == m2, lane, big), axis=1, keepdims=True)
    e2 = jnp.exp(m2 - m1)
    den = 1.0 + e2
    return jnp.where(lane == i1, 1.0 / den, 0.0) + jnp.where(lane == i2, e2 / den, 0.0)


def _ffn_body(x_ref, p_ref, w1_ref, w3_ref, w2_ref, rw_ref, pg_ref, pp_ref, g_ref, b_ref, o_ref,
              xb_scr, acc_scr, comb_scr, *, alpha, moe):
    e = pl.program_id(1)
    f = pl.program_id(2)
    first = jnp.logical_and(e == 0, f == 0)
    last = jnp.logical_and(e == pl.num_programs(1) - 1, f == pl.num_programs(2) - 1)

    @pl.when(first)
    def _():
        x = x_ref[...]
        xb_scr[...] = x.astype(BF16)
        acc_scr[...] = jnp.zeros(acc_scr.shape, F32)
        if moe:
            comb_scr[...] = _top2_combine(x, rw_ref)

    xb = xb_scr[...]
    h1 = jnp.dot(xb, w1_ref[...], preferred_element_type=F32)
    h3 = jnp.dot(xb, w3_ref[...], preferred_element_type=F32)
    hh = (jax.nn.silu(h1) * h3).astype(BF16)
    part = jnp.dot(hh, w2_ref[...], preferred_element_type=F32)
    if moe:
        comb = comb_scr[...]
        lane = lax.broadcasted_iota(jnp.int32, comb.shape, 1)
        part = part * jnp.sum(jnp.where(lane == e, comb, 0.0), axis=1, keepdims=True)
    acc_scr[...] += part

    @pl.when(last)
    def _():
        gate = jax.nn.sigmoid(jnp.dot(xb, pg_ref[...], preferred_element_type=F32))
        ple = gate * jnp.dot(p_ref[...].astype(BF16), pp_ref[...], preferred_element_type=F32)
        y = alpha * x_ref[...] + acc_scr[...] + ple
        o_ref[...] = _layer_norm_rows(y, g_ref[...], b_ref[...])


def _ffn(x, p, w1_b, w3_b, w2_b, rw, pg_b, pp_b, g, b, tm, tf, alpha, moe):
    m = x.shape[0]
    n_e, _, d_ff = w1_b.shape
    vec = pl.BlockSpec((1, D_MODEL), lambda i, e, f: (0, 0))
    return pl.pallas_call(
        functools.partial(_ffn_body, alpha=alpha, moe=moe),
        grid=(m // tm, n_e, d_ff // tf),
        in_specs=[pl.BlockSpec((tm, D_MODEL), lambda i, e, f: (i, 0)),
                  pl.BlockSpec((tm, PLE_DIM), lambda i, e, f: (i, 0)),
                  pl.BlockSpec((None, D_MODEL, tf), lambda i, e, f: (e, 0, f)),
                  pl.BlockSpec((None, D_MODEL, tf), lambda i, e, f: (e, 0, f)),
                  pl.BlockSpec((None, tf, D_MODEL), lambda i, e, f: (e, f, 0)),
                  pl.BlockSpec((D_MODEL, LANES), lambda i, e, f: (0, 0)),
                  pl.BlockSpec((D_MODEL, D_MODEL), lambda i, e, f: (0, 0)),
                  pl.BlockSpec((PLE_DIM, D_MODEL), lambda i, e, f: (0, 0)),
                  vec, vec],
        out_specs=pl.BlockSpec((tm, D_MODEL), lambda i, e, f: (i, 0)),
        out_shape=_sds((m, D_MODEL), F32),
        scratch_shapes=[pltpu.VMEM((tm, D_MODEL), BF16),
                        pltpu.VMEM((tm, D_MODEL), F32),
                        pltpu.VMEM((tm, LANES), F32)],
        compiler_params=_cparams(("arbitrary", "arbitrary", "arbitrary")),
        name="ffn_moe" if moe else "ffn_dense",
    )(x, p, w1_b, w3_b, w2_b, rw, pg_b, pp_b, g, b)


def _row_tile(m, pref):
    return pref if m % pref == 0 else m


def _trunk(x, p, conv_state, h_state, bsz, seq, w, attend):
    depth = w['w_in'].shape[0]
    m = bsz * seq
    alpha = (2.0 * depth) ** 0.25
    tm = _row_tile(m, 512)
    tt = _row_tile(seq, 256)
    xf = x.reshape(m, D_MODEL)
    ks, vs, hs, convs = [], [], [], []
    for l in range(depth):
        lam_init = 0.8 - 0.6 * math.exp(-0.3 * l)
        z, zb = _inproj(xf, w['w_in'][l], tm)
        y_rnn, h_last = _rnn(z, conv_state[l], h_state[l].reshape(bsz, 1, D_RNN), w['conv_w'][l],
                             w['conv_b'][l].reshape(1, D_RNN), w['lru_wr'][l],
                             w['lru_br'][l].reshape(1, D_RNN), w['lru_wi'][l],
                             w['lru_bi'][l].reshape(1, D_RNN), w['lru_lambda'][l].reshape(1, D_RNN),
                             bsz, seq, tt)
        y_att = attend(l, z, zb, w['lamp'][l], w['subln_g'][l], lam_init)
        x1 = _merge(y_rnn, y_att, z, xf, w['w_rnn_out'][l], w['w_attn_out'][l], w['w_o'][l],
                    w['ln1_g'][l].reshape(1, D_MODEL), w['ln1_b'][l].reshape(1, D_MODEL), tm, alpha)
        mi = l // 2
        if l % 2 == 0:
            w1, w3, w2 = (w['ffn_w1'][mi][None], w['ffn_w3'][mi][None], w['ffn_w2'][mi][None])
            rw = w['router_pad'][0]
        else:
            w1, w3, w2 = w['moe_w1'][mi], w['moe_w3'][mi], w['moe_w2'][mi]
            rw = w['router_pad'][mi]
        xf = _ffn(x1, p[l].reshape(m, PLE_DIM), w1, w3, w2, rw, w['ple_gate_w'][l], w['ple_proj_w'][l],
                  w['ln2_g'][l].reshape(1, D_MODEL), w['ln2_b'][l].reshape(1, D_MODEL),
                  tm, w1.shape[2] // 2, alpha, moe=(l % 2 == 1))
        ks.append(z[3].reshape(bsz, seq, ATT_HEADS, ATT_VDIM))
        vs.append(z[4].reshape(bsz, seq, ATT_HEADS, ATT_VDIM))
        hs.append(h_last.reshape(bsz, D_RNN))
        xr = z[0].reshape(bsz, seq, D_RNN)
        xpad = jnp.concatenate([conv_state[l].astype(xr.dtype), xr], axis=1)
        convs.append(xpad[:, -(CONV_W - 1):])
    return (xf.reshape(bsz, seq, D_MODEL), jnp.stack(ks), jnp.stack(vs), jnp.stack(hs), jnp.stack(convs))


def kernel(x_prompt, x_sample, cache_k, cache_v, state_h, state_conv, page_table, p_prompt, p_sample, rel_bias, w_in, conv_w, conv_b, lru_wr, lru_br, lru_wi, lru_bi, lru_lambda, w_rnn_out, w_attn_out, lam_q1, lam_k1, lam_q2, lam_k2, subln_g, w_o, ln1_g, ln1_b, ffn_w1, ffn_w3, ffn_w2, router_w, moe_w1, moe_w3, moe_w2, ple_gate_w, ple_proj_w, ln2_g, ln2_b):
    depth = w_in.shape[0]
    bsz, seq, _ = x_prompt.shape
    dbsz, dseq, _ = x_sample.shape
    n_pool, page_size = cache_k.shape[1], cache_k.shape[2]
    router_pad = jnp.zeros((max(router_w.shape[0], 1), D_MODEL, LANES), F32)
    if router_w.shape[0]:
        router_pad = router_pad.at[:, :, :N_EXPERTS].set(router_w.astype(F32))
    w = dict(
        w_in=w_in.astype(BF16), conv_w=conv_w, conv_b=conv_b,
        lru_wr=jax.vmap(_block_diag_groups)(lru_wr).astype(BF16), lru_br=lru_br,
        lru_wi=jax.vmap(_block_diag_groups)(lru_wi).astype(BF16), lru_bi=lru_bi,
        lru_lambda=lru_lambda, w_rnn_out=w_rnn_out.astype(BF16), w_attn_out=w_attn_out.astype(BF16),
        lamp=jnp.stack([lam_q1, lam_k1, lam_q2, lam_k2], axis=1).astype(F32), subln_g=subln_g.astype(F32),
        w_o=w_o.astype(BF16), ln1_g=ln1_g, ln1_b=ln1_b,
        ffn_w1=ffn_w1.astype(BF16), ffn_w3=ffn_w3.astype(BF16), ffn_w2=ffn_w2.astype(BF16),
        router_pad=router_pad, moe_w1=moe_w1.astype(BF16), moe_w3=moe_w3.astype(BF16),
        moe_w2=moe_w2.astype(BF16), ple_gate_w=ple_gate_w.astype(BF16),
        ple_proj_w=ple_proj_w.astype(BF16), ln2_g=ln2_g, ln2_b=ln2_b)

    tq = _row_tile(seq, 256)
    pbias = _prompt_bias(rel_bias, tq)

    def prompt_attend(l, z, zb, lamp, g, lam_init):
        del l, z
        return _prompt_attend(zb, pbias, lamp, g.reshape(1, ATT_VDIM), bsz, seq, tq, lam_init)

    conv0 = jnp.zeros((depth, bsz, CONV_W - 1, D_RNN), x_prompt.dtype)
    h0 = jnp.zeros((depth, bsz, D_RNN), F32)
    y_prompt, k_prompt, v_prompt, h_prompt, conv_prompt = _trunk(
        x_prompt, p_prompt, conv0, h0, bsz, seq, w, prompt_attend)

    rows = page_size * ATT_HEADS
    cache_k4 = cache_k.reshape(depth, n_pool, rows, ATT_VDIM)
    cache_v4 = cache_v.reshape(depth, n_pool, rows, ATT_VDIM)
    mask, bfar, blast, bnew = _sample_tables(rel_bias, page_size, dseq)
    n_pages = page_table.shape[1]
    n_pp = 8 if n_pages % 8 == 0 else (4 if n_pages % 4 == 0 else 1)

    def sample_attend(l, z, zb, lamp, g, lam_init):
        del zb
        wq = _sample_wq(z[2], dbsz, dseq)
        knew = z[3].reshape(dbsz, dseq * ATT_HEADS, ATT_VDIM)
        vnew = z[4].reshape(dbsz, dseq * ATT_HEADS, ATT_VDIM)
        o = _sample_attend(l, page_table, cache_k4, cache_v4, wq, knew, vnew, bfar, blast, bnew, mask,
                           lamp, g.reshape(ATT_VDIM, 1), n_pp, lam_init)
        o = o.reshape(dbsz, ATT_VDIM, ATT_HEADS, dseq)
        return jnp.transpose(o, (0, 3, 2, 1)).reshape(dbsz * dseq, ATT_HEADS * ATT_VDIM).astype(BF16)

    y_sample, k_sample, v_sample, h_sample, conv_sample = _trunk(
        x_sample, p_sample, state_conv, state_h, dbsz, dseq, w, sample_attend)
    return (y_prompt, y_sample, k_prompt, v_prompt, h_prompt, conv_prompt,
            k_sample, v_sample, h_sample, conv_sample)
```

```python
import functools
import math

import jax
import jax.numpy as jnp
from jax import lax
from jax.experimental import pallas as pl
from jax.experimental.pallas import tpu as pltpu

F32 = jnp.float32
BF16 = jnp.bfloat16

D_MODEL = 1024
D_RNN = 1024
RNN_HEADS = 16
RNN_BLOCK = D_RNN // RNN_HEADS
CONV_W = 4
LRU_C = 8.0
ATT_HEADS = 8
ATT_DH = 64
ATT_VDIM = 2 * ATT_DH
N_PLANES = 7
N_BUCKETS = 32
MAX_EXACT = N_BUCKETS // 2
MAX_DIST = 128
N_EXPERTS = 8
PLE_DIM = 256
LN_EPS = 1e-5
NEG_INF = -1e30
QK_SCALE = ATT_DH ** -0.5
LOG2E = math.log2(math.e)
VT_PAD = 16
PROMPT_HEADS_PER_STEP = 2

LANES = 128
SUBLANES = 8
MXU_DIM = 256
VMEM_LIMIT = 56 << 20


def _cparams(sem):
    return pltpu.CompilerParams(dimension_semantics=sem, vmem_limit_bytes=VMEM_LIMIT)


def _sds(shape, dtype):
    return jax.ShapeDtypeStruct(shape, dtype)


def _layer_norm_rows(y, g, b):
    mu = jnp.mean(y, axis=-1, keepdims=True)
    yc = y - mu
    var = jnp.mean(yc * yc, axis=-1, keepdims=True)
    return yc * lax.rsqrt(var + LN_EPS) * g + b


def _inproj_body(x_ref, w_ref, z_ref, zb_ref, xb_scr):
    j = pl.program_id(1)

    @pl.when(j == 0)
    def _():
        xb_scr[...] = x_ref[...].astype(BF16)

    acc = jnp.dot(xb_scr[...], w_ref[...], preferred_element_type=F32)
    z_ref[...] = acc

    @pl.when(j == 2)
    def _():
        zb_ref[...] = (acc * (QK_SCALE * LOG2E)).astype(BF16)

    @pl.when(jnp.logical_or(j == 3, j == 4))
    def _():
        zb_ref[...] = acc.astype(BF16)


def _inproj(x, w_b, tm):
    m = x.shape[0]
    return pl.pallas_call(
        _inproj_body,
        grid=(m // tm, N_PLANES),
        in_specs=[pl.BlockSpec((tm, D_MODEL), lambda i, j: (i, 0)),
                  pl.BlockSpec((D_MODEL, D_MODEL), lambda i, j: (0, j))],
        out_specs=[pl.BlockSpec((None, tm, D_MODEL), lambda i, j: (j, i, 0)),
                   pl.BlockSpec((None, tm, D_MODEL), lambda i, j: (jnp.clip(j - 2, 0, 2), i, 0))],
        out_shape=[_sds((N_PLANES, m, D_MODEL), F32), _sds((3, m, D_MODEL), BF16)],
        scratch_shapes=[pltpu.VMEM((tm, D_MODEL), BF16)],
        compiler_params=_cparams(("arbitrary", "arbitrary")),
        name="inproj",
    )(x, w_b)


def _rnn_body(xr_ref, gr_ref, conv0_ref, h0_ref, cw_ref, cb_ref, wr_ref, br_ref, wi_ref, bi_ref,
              lam_ref, y_ref, hlast_ref, xpad_scr, a_scr, b_scr, h_scr, hs_scr):
    t = pl.program_id(1)
    tt = xr_ref.shape[0]
    halo = SUBLANES

    @pl.when(t == 0)
    def _():
        xpad_scr[halo - 3:halo, :] = conv0_ref[...]
        h_scr[...] = h0_ref[...]

    xr = xr_ref[...]
    xpad_scr[halo:halo + tt, :] = xr
    cw = cw_ref[...]
    xc = cb_ref[...] + xpad_scr[halo - 3:halo - 3 + tt, :] * cw[0:1]
    xc = xc + xpad_scr[halo - 2:halo - 2 + tt, :] * cw[1:2]
    xc = xc + xpad_scr[halo - 1:halo - 1 + tt, :] * cw[2:3]
    xc = xc + xr * cw[3:4]
    tail = xpad_scr[halo + tt - 3:halo + tt, :]
    xpad_scr[halo - 3:halo, :] = tail

    xcb = xc.astype(BF16)
    n_grp = D_RNN // MXU_DIM
    r_parts, i_parts = [], []
    for g in range(n_grp):
        xg = xcb[:, g * MXU_DIM:(g + 1) * MXU_DIM]
        r_parts.append(jnp.dot(xg, wr_ref[g], preferred_element_type=F32))
        i_parts.append(jnp.dot(xg, wi_ref[g], preferred_element_type=F32))
    r = jax.nn.sigmoid(jnp.concatenate(r_parts, axis=1) + br_ref[...])
    ig = jax.nn.sigmoid(jnp.concatenate(i_parts, axis=1) + bi_ref[...])
    log_a = -LRU_C * r * jax.nn.softplus(-lam_ref[...])
    a = jnp.exp(log_a)
    a_scr[...] = a
    b_scr[...] = jnp.sqrt(-jnp.tanh(log_a) * (a * a + 1.0)) * (ig * xc)

    def step(i, h):
        h = a_scr[pl.ds(i, 1), :] * h + b_scr[pl.ds(i, 1), :]
        hs_scr[pl.ds(i, 1), :] = h
        return h

    h = lax.fori_loop(0, tt, step, h_scr[...], unroll=8)
    h_scr[...] = h
    y_ref[...] = (jax.nn.gelu(gr_ref[...]) * hs_scr[...]).astype(BF16)

    @pl.when(t == pl.num_programs(1) - 1)
    def _():
        hlast_ref[...] = h


def _rnn(z, conv0, h0, cw, cb, wr_b, br, wi_b, bi, lam, bsz, seq, tt):
    m = bsz * seq
    nt = seq // tt
    n_grp = D_RNN // MXU_DIM
    row = lambda b, t: (0, 0)
    return pl.pallas_call(
        _rnn_body,
        grid=(bsz, nt),
        in_specs=[pl.BlockSpec((None, tt, D_RNN), lambda b, t: (0, b * nt + t, 0)),
                  pl.BlockSpec((None, tt, D_RNN), lambda b, t: (1, b * nt + t, 0)),
                  pl.BlockSpec((None, CONV_W - 1, D_RNN), lambda b, t: (b, 0, 0)),
                  pl.BlockSpec((None, 1, D_RNN), lambda b, t: (b, 0, 0)),
                  pl.BlockSpec((CONV_W, D_RNN), row),
                  pl.BlockSpec((1, D_RNN), row),
                  pl.BlockSpec((n_grp, MXU_DIM, MXU_DIM), lambda b, t: (0, 0, 0)),
                  pl.BlockSpec((1, D_RNN), row),
                  pl.BlockSpec((n_grp, MXU_DIM, MXU_DIM), lambda b, t: (0, 0, 0)),
                  pl.BlockSpec((1, D_RNN), row),
                  pl.BlockSpec((1, D_RNN), row)],
        out_specs=[pl.BlockSpec((tt, D_RNN), lambda b, t: (b * nt + t, 0)),
                   pl.BlockSpec((None, 1, D_RNN), lambda b, t: (b, 0, 0))],
        out_shape=[_sds((m, D_RNN), BF16), _sds((bsz, 1, D_RNN), F32)],
        scratch_shapes=[pltpu.VMEM((tt + SUBLANES, D_RNN), F32),
                        pltpu.VMEM((tt, D_RNN), F32),
                        pltpu.VMEM((tt, D_RNN), F32),
                        pltpu.VMEM((1, D_RNN), F32),
                        pltpu.VMEM((tt, D_RNN), F32)],
        compiler_params=_cparams(("arbitrary", "arbitrary")),
        name="rnn",
    )(z, z, conv0, h0, cw, cb, wr_b, br, wi_b, bi, lam)


def _block_diag_groups(w):
    per = MXU_DIM // RNN_BLOCK
    n_grp = RNN_HEADS // per
    wg = w.reshape(n_grp, per, RNN_BLOCK, RNN_BLOCK)
    eye = jnp.eye(per, dtype=w.dtype)
    out = jnp.einsum('gpij,pq->gpiqj', wg, eye)
    return out.reshape(n_grp, MXU_DIM, MXU_DIM)


def _bucket(n):
    n_f = jnp.maximum(n, 1).astype(F32)
    large = MAX_EXACT + (jnp.log(n_f / MAX_EXACT) / math.log(MAX_DIST / MAX_EXACT)
                         * (N_BUCKETS - MAX_EXACT)).astype(jnp.int32)
    return jnp.where(n < MAX_EXACT, n, jnp.minimum(large, N_BUCKETS - 1))


def _lambda_value(lamp_ref, lam_init):
    lp = lamp_ref[...]
    s1 = jnp.sum(lp[0:1] * lp[1:2], axis=-1, keepdims=True)
    s2 = jnp.sum(lp[2:3] * lp[3:4], axis=-1, keepdims=True)
    return jnp.exp(s1) - jnp.exp(s2) + lam_init


def _table_lookup(tab, bucket):
    ids = jnp.arange(N_BUCKETS, dtype=bucket.dtype)[:, None]
    return jnp.sum(jnp.where(bucket[..., None, None] == ids, tab, 0.0), axis=-2)


def _prompt_bias(rel_bias, tq):
    j = jnp.arange(tq)[:, None]
    i = jnp.arange(tq)[None, :]
    tab = rel_bias.astype(F32)
    tab = tab - tab[N_BUCKETS - 1]
    prev = jnp.moveaxis(_table_lookup(tab, _bucket(tq + i - j)), -1, 0)
    diag = jnp.moveaxis(_table_lookup(tab, _bucket(jnp.maximum(i - j, 0))), -1, 0)
    diag = jnp.where((j <= i)[None], diag * LOG2E, NEG_INF)
    b4 = jnp.stack([jnp.zeros_like(prev), prev * LOG2E, diag, jnp.full_like(prev, NEG_INF)], axis=1)
    return jnp.concatenate([b4, b4], axis=3)


def _pattn_body(q_ref, k_ref, v_ref, bias_ref, lamp_ref, g_ref, o_ref, vt_scr, s_scr, acc_scr,
                *, lam_init, n_hd):
    qi = pl.program_id(2)
    tq = q_ref.shape[0]
    tk = vt_scr.shape[-1]
    n_blk = k_ref.shape[0] // tk

    @pl.when(qi == 0)
    def _():
        def transpose_values(c, carry):
            start = pl.multiple_of(c * tk, tk)
            for hh in range(n_hd):
                vb = v_ref[pl.ds(start, tk), hh * ATT_VDIM:(hh + 1) * ATT_VDIM].astype(F32)
                vt_scr[hh, c, :ATT_VDIM, :] = vb.T.astype(BF16)
                vt_scr[hh, c, ATT_VDIM:, :] = jnp.ones((VT_PAD, tk), BF16)
            return carry

        lax.fori_loop(0, n_blk, transpose_values, 0)

    row = lax.broadcasted_iota(jnp.int32, (ATT_VDIM, tq), 0)
    qqt = []
    for hh in range(n_hd):
        qt = q_ref[:, hh * ATT_VDIM:(hh + 1) * ATT_VDIM].astype(F32).T
        qqt.append(jnp.concatenate([jnp.where(row < ATT_DH, qt, 0.0), jnp.where(row >= ATT_DH, qt, 0.0)],
                                   axis=1).astype(BF16))

    def scores(hh, ki):
        start = pl.multiple_of(jnp.minimum(ki, qi) * tk, tk)
        kb = k_ref[pl.ds(start, tk), hh * ATT_VDIM:(hh + 1) * ATT_VDIM]
        return jnp.dot(kb, qqt[hh], preferred_element_type=F32)

    def consume(hh, slot, ki, m):
        sel = jnp.clip(ki - (qi - 2), 0, 3)
        st = s_scr[hh, slot] + bias_ref[hh, sel]
        m_new = jnp.maximum(m, jnp.max(st, axis=0, keepdims=True))
        alpha = jnp.exp2(m - m_new)
        p = jnp.exp2(st - m_new)
        pv = jnp.dot(vt_scr[hh, jnp.minimum(ki, qi)], p.astype(BF16), preferred_element_type=F32)
        acc_scr[hh] = alpha * acc_scr[hh] + pv
        return m_new

    def pair_step(pi, stats):
        k0 = 2 * pi
        stats = list(stats)
        for hh in range(n_hd):
            s_scr[hh, 1] = scores(hh, k0 + 1)
        for hh in range(n_hd):
            stats[hh] = consume(hh, 0, k0, stats[hh])
        for hh in range(n_hd):
            s_scr[hh, 0] = scores(hh, k0 + 2)
        for hh in range(n_hd):
            stats[hh] = consume(hh, 1, k0 + 1, stats[hh])
        return tuple(stats)

    for hh in range(n_hd):
        s_scr[hh, 0] = scores(hh, 0)
        acc_scr[hh] = jnp.zeros(acc_scr.shape[1:], F32)
    init = tuple(jnp.full((1, 2 * tq), NEG_INF, F32) for _ in range(n_hd))
    lax.fori_loop(0, lax.shift_right_logical(qi + 2, 1), pair_step, init)
    lam = _lambda_value(lamp_ref, lam_init)
    for hh in range(n_hd):
        acc = acc_scr[hh]
        a = acc[:ATT_VDIM] / acc[ATT_VDIM:ATT_VDIM + 1]
        o = a[:, :tq] - lam * a[:, tq:]
        o = o * lax.rsqrt(jnp.mean(o * o, axis=0, keepdims=True) + LN_EPS) * g_ref[...] * (1.0 - lam_init)
        o_ref[:, hh * ATT_VDIM:(hh + 1) * ATT_VDIM] = o.T.astype(BF16)


def _prompt_attend(zb, bias, lamp, g_col, bsz, seq, tq, lam_init):
    m = bsz * seq
    nq = seq // tq
    n_hd = PROMPT_HEADS_PER_STEP
    wd = n_hd * ATT_VDIM
    return pl.pallas_call(
        functools.partial(_pattn_body, lam_init=lam_init, n_hd=n_hd),
        grid=(bsz, ATT_HEADS // n_hd, nq),
        in_specs=[pl.BlockSpec((None, tq, wd), lambda b, h, i: (0, b * nq + i, h)),
                  pl.BlockSpec((None, seq, wd), lambda b, h, i: (1, b, h)),
                  pl.BlockSpec((None, seq, wd), lambda b, h, i: (2, b, h)),
                  pl.BlockSpec((n_hd, 4, tq, 2 * tq), lambda b, h, i: (h, 0, 0, 0)),
                  pl.BlockSpec((4, ATT_DH), lambda b, h, i: (0, 0)),
                  pl.BlockSpec((ATT_VDIM, 1), lambda b, h, i: (0, 0))],
        out_specs=pl.BlockSpec((tq, wd), lambda b, h, i: (b * nq + i, h)),
        out_shape=_sds((m, ATT_HEADS * ATT_VDIM), BF16),
        scratch_shapes=[pltpu.VMEM((n_hd, seq // tq, ATT_VDIM + VT_PAD, tq), BF16),
                        pltpu.VMEM((n_hd, 2, tq, 2 * tq), F32),
                        pltpu.VMEM((n_hd, ATT_VDIM + VT_PAD, 2 * tq), F32)],
        compiler_params=_cparams(("arbitrary", "arbitrary", "arbitrary")),
        name="prompt_attn",
    )(zb, zb, zb, bias, lamp, g_col)


def _sattn_body(pt_ref, *refs, n_pp, lam_init):
    del pt_ref
    kp_refs = refs[:n_pp]
    vp_refs = refs[n_pp:2 * n_pp]
    (wq_ref, knew_ref, vnew_ref, blast_ref, bnew_ref, mask_ref, lamp_ref, g_ref,
     o_ref, m_scr, l_scr, acc_scr, s_scr) = refs[2 * n_pp:]
    c = pl.program_id(1)
    last = c == pl.num_programs(1) - 1
    rows = kp_refs[0].shape[0]
    n_tok = rows // ATT_HEADS

    @pl.when(c == 0)
    def _():
        m_scr[...] = jnp.full(m_scr.shape, NEG_INF, F32)
        l_scr[...] = jnp.zeros(l_scr.shape, F32)
        acc_scr[...] = jnp.zeros(acc_scr.shape, F32)

    valid = mask_ref[...] > 0.0

    def update(state, s3, vs):
        m_old, l, acc = state
        n_grp = len(vs)
        m2 = jnp.max(s3, axis=0)
        mx = m2[:, :LANES]
        for g in range(1, n_grp):
            mx = jnp.maximum(mx, m2[:, g * LANES:(g + 1) * LANES])
        m_new = jnp.maximum(m_old, mx)
        alpha = jnp.exp2(m_old - m_new)
        m_sub = jnp.where(valid, m_new, -NEG_INF)
        p = jnp.exp2(s3 - jnp.concatenate([m_sub] * n_grp, axis=1)[None])
        l2 = jnp.sum(p, axis=0)
        p2 = p.reshape(vs[0].shape[0], n_grp * LANES).astype(BF16)
        l = alpha * l
        acc = acc * jnp.sum(jnp.where(valid, alpha, 0.0), axis=0, keepdims=True)
        for g, vv in enumerate(vs):
            l = l + l2[:, g * LANES:(g + 1) * LANES]
            acc = acc + lax.dot_general(vv, p2[:, g * LANES:(g + 1) * LANES], (((0,), (0,)), ((), ())),
                                        preferred_element_type=F32)
        return m_new, l, acc

    wq2 = wq_ref[...]

    def pair_scores(pp):
        k2 = jnp.concatenate([kp_refs[2 * pp][...].astype(BF16), kp_refs[2 * pp + 1][...].astype(BF16)], axis=1)
        return jnp.dot(k2, wq2, preferred_element_type=F32)

    n_pairs = n_pp // 2
    state = (m_scr[...], l_scr[...], acc_scr[...])
    s_scr[0] = pair_scores(0)
    for pp in range(n_pairs):
        if pp + 1 < n_pairs:
            s_scr[(pp + 1) % 2] = pair_scores(pp + 1)
        s3 = s_scr[pp % 2].reshape(n_tok, ATT_HEADS, 2 * LANES)
        if pp == n_pairs - 1:
            s3 = s3 + jnp.where(last, blast_ref[...].reshape(n_tok, ATT_HEADS, 2 * LANES), 0.0)
        state = update(state, s3, [vp_refs[2 * pp][...].astype(BF16), vp_refs[2 * pp + 1][...].astype(BF16)])
    m_scr[...], l_scr[...], acc_scr[...] = state

    @pl.when(last)
    def _():
        n_new = knew_ref.shape[0] // ATT_HEADS
        sn = jnp.dot(knew_ref[...].astype(BF16), wq_ref[:LANES, :LANES], preferred_element_type=F32)
        sn = sn + bnew_ref[...]
        _, l, acc = update(state, sn.reshape(n_new, ATT_HEADS, LANES), [vnew_ref[...].astype(BF16)])
        l_row = jnp.sum(jnp.where(valid, l, 0.0), axis=0, keepdims=True)
        a = acc / l_row
        lam = _lambda_value(lamp_ref, lam_init)
        half = LANES // 2
        o = a[:, :half] - lam * a[:, half:]
        o = o * lax.rsqrt(jnp.mean(o * o, axis=0, keepdims=True) + LN_EPS) * g_ref[...] * (1.0 - lam_init)
        o_ref[...] = o


def _sample_attend(layer, page_table, cache_k4, cache_v4, wq2, knew, vnew, blast, bnew, mask,
                   lamp, g_col, n_pp, lam_init):
    bsz, n_pages = page_table.shape
    rows = cache_k4.shape[2]
    n_new = knew.shape[1]
    nc = n_pages // n_pp

    def page_spec(p):
        return pl.BlockSpec((None, None, rows, LANES),
                            lambda b, c, pt: (layer, pt[b, c * n_pp + p], 0, 0))

    const2 = lambda b, c, pt: (0, 0)
    in_specs = ([page_spec(p) for p in range(n_pp)] + [page_spec(p) for p in range(n_pp)] + [
        pl.BlockSpec((None, 2 * LANES, 2 * LANES), lambda b, c, pt: (b, 0, 0)),
        pl.BlockSpec((None, n_new, LANES), lambda b, c, pt: (b, 0, 0)),
        pl.BlockSpec((None, n_new, LANES), lambda b, c, pt: (b, 0, 0)),
        pl.BlockSpec((rows, 2 * LANES), const2),
        pl.BlockSpec((n_new, LANES), const2),
        pl.BlockSpec((SUBLANES, LANES), const2),
        pl.BlockSpec((4, ATT_DH), const2),
        pl.BlockSpec((ATT_VDIM, 1), const2)])
    grid_spec = pltpu.PrefetchScalarGridSpec(
        num_scalar_prefetch=1,
        grid=(bsz, nc),
        in_specs=in_specs,
        out_specs=pl.BlockSpec((None, ATT_VDIM, LANES // 2), lambda b, c, pt: (b, 0, 0)),
        scratch_shapes=[pltpu.VMEM((SUBLANES, LANES), F32),
                        pltpu.VMEM((SUBLANES, LANES), F32),
                        pltpu.VMEM((ATT_VDIM, LANES), F32),
                        pltpu.VMEM((2, rows, 2 * LANES), F32)])
    return pl.pallas_call(
        functools.partial(_sattn_body, n_pp=n_pp, lam_init=lam_init),
        grid_spec=grid_spec,
        out_shape=_sds((bsz, ATT_VDIM, LANES // 2), F32),
        compiler_params=_cparams(("arbitrary", "arbitrary")),
        name="sample_attn",
    )(page_table, *([cache_k4] * n_pp), *([cache_v4] * n_pp), wq2, knew, vnew, blast, bnew, mask,
      lamp, g_col)


def _sample_tables(rel_bias, page_size, dec_seq):
    tab = rel_bias.astype(F32)
    tab = (tab - tab[N_BUCKETS - 1]) * LOG2E
    col = jnp.arange(LANES)
    col_h = (col % (LANES // 2)) // dec_seq
    col_q = col % dec_seq
    row_h = jnp.arange(ATT_HEADS)
    mask = (row_h[:, None] == col_h[None, :]).astype(F32)
    t = jnp.arange(page_size)
    dist = page_size + col_q[None, :] - t[:, None]
    blast = _table_lookup(tab, _bucket(dist))
    blast = jnp.moveaxis(blast, -1, 1).reshape(page_size * ATT_HEADS, LANES)
    blast = jnp.concatenate([jnp.zeros_like(blast), blast], axis=1)
    j = jnp.arange(dec_seq)
    dn = col_q[None, :] - j[:, None]
    bnew = jnp.where((dn >= 0)[:, None, :],
                     jnp.moveaxis(_table_lookup(tab, _bucket(jnp.maximum(dn, 0))), -1, 1), NEG_INF)
    bnew = bnew.reshape(dec_seq * ATT_HEADS, LANES)
    return mask, blast, bnew


def _sample_wq(q_plane, bsz, dec_seq):
    q = q_plane.reshape(bsz, dec_seq, ATT_HEADS, 2, ATT_DH) * (QK_SCALE * LOG2E)
    eye = jnp.eye(2, dtype=q.dtype)
    w = jnp.einsum('bqhcd,ce->bcdehq', q, eye).reshape(bsz, LANES, LANES)
    w2 = jnp.einsum('bfc,pq->bpfqc', w, eye)
    return w2.reshape(bsz, 2 * LANES, 2 * LANES).astype(BF16)


def _merge_body(yr_ref, ya_ref, ga_ref, gb_ref, x_ref, wr_ref, wa_ref, wo_ref, g_ref, b_ref, o_ref,
                *, alpha):
    pr = jnp.dot(yr_ref[...], wr_ref[...], preferred_element_type=F32)
    pa = jnp.dot(ya_ref[...], wa_ref[...], preferred_element_type=F32)
    merged = jax.nn.sigmoid(ga_ref[...]) * pr + jax.nn.sigmoid(gb_ref[...]) * pa
    mix = jnp.dot(merged.astype(BF16), wo_ref[...], preferred_element_type=F32)
    o_ref[...] = _layer_norm_rows(alpha * x_ref[...] + mix, g_ref[...], b_ref[...])


def _merge(yr, ya, z, x, wr_b, wa_b, wo_b, g, b, tm, alpha):
    m = x.shape[0]
    rows = pl.BlockSpec((tm, D_MODEL), lambda i: (i, 0))
    full = pl.BlockSpec((D_MODEL, D_MODEL), lambda i: (0, 0))
    vec = pl.BlockSpec((1, D_MODEL), lambda i: (0, 0))
    return pl.pallas_call(
        functools.partial(_merge_body, alpha=alpha),
        grid=(m // tm,),
        in_specs=[rows, rows,
                  pl.BlockSpec((None, tm, D_MODEL), lambda i: (5, i, 0)),
                  pl.BlockSpec((None, tm, D_MODEL), lambda i: (6, i, 0)),
                  rows, full, full, full, vec, vec],
        out_specs=rows,
        out_shape=_sds((m, D_MODEL), F32),
        compiler_params=_cparams(("arbitrary",)),
        name="merge",
    )(yr, ya, z, z, x, wr_b, wa_b, wo_b, g, b)


def _top2_combine(x, rw_ref):
    logits = jnp.dot(x, rw_ref[...], preferred_element_type=F32, precision=lax.Precision.HIGHEST)
    lane = lax.broadcasted_iota(jnp.int32, logits.shape, 1).astype(F32)
    big = float(LANES)
    lg = jnp.where(lane < N_EXPERTS, logits, -jnp.inf)
    m1 = jnp.max(lg, axis=1, keepdims=True)
    i1 = jnp.min(jnp.where(lg == m1, lane, big), axis=1, keepdims=True)
    lg2 = jnp.where(lane == i1, -jnp.inf, lg)
    m2 = jnp.max(lg2, axis=1, keepdims=True)
    i2 = jnp.min(jnp.where(lg2 == m2, lane, big), axis=1, keepdims=True)
    e2 = jnp.exp(m2 - m1)
    den = 1.0 + e2
    return jnp.where(lane == i1, 1.0 / den, 0.0) + jnp.where(lane == i2, e2 / den, 0.0)


def _ffn_body(x_ref, p_ref, w1_ref, w3_ref, w2_ref, rw_ref, pg_ref, pp_ref, g_ref, b_ref, o_ref,
              xb_scr, acc_scr, comb_scr, *, alpha, moe):
    e = pl.program_id(1)
    f = pl.program_id(2)
    first = jnp.logical_and(e == 0, f == 0)
    last = jnp.logical_and(e == pl.num_programs(1) - 1, f == pl.num_programs(2) - 1)

    @pl.when(first)
    def _():
        x = x_ref[...]
        xb_scr[...] = x.astype(BF16)
        acc_scr[...] = jnp.zeros(acc_scr.shape, F32)
        if moe:
            comb_scr[...] = _top2_combine(x, rw_ref)

    xb = xb_scr[...]
    h1 = jnp.dot(xb, w1_ref[...], preferred_element_type=F32)
    h3 = jnp.dot(xb, w3_ref[...], preferred_element_type=F32)
    hh = (jax.nn.silu(h1) * h3).astype(BF16)
    part = jnp.dot(hh, w2_ref[...], preferred_element_type=F32)
    if moe:
        comb = comb_scr[...]
        lane = lax.broadcasted_iota(jnp.int32, comb.shape, 1)
        part = part * jnp.sum(jnp.where(lane == e, comb, 0.0), axis=1, keepdims=True)
    acc_scr[...] += part

    @pl.when(last)
    def _():
        gate = jax.nn.sigmoid(jnp.dot(xb, pg_ref[...], preferred_element_type=F32))
        ple = gate * jnp.dot(p_ref[...].astype(BF16), pp_ref[...], preferred_element_type=F32)
        y = alpha * x_ref[...] + acc_scr[...] + ple
        o_ref[...] = _layer_norm_rows(y, g_ref[...], b_ref[...])


def _ffn(x, p, w1_b, w3_b, w2_b, rw, pg_b, pp_b, g, b, tm, tf, alpha, moe):
    m = x.shape[0]
    n_e, _, d_ff = w1_b.shape
    vec = pl.BlockSpec((1, D_MODEL), lambda i, e, f: (0, 0))
    return pl.pallas_call(
        functools.partial(_ffn_body, alpha=alpha, moe=moe),
        grid=(m // tm, n_e, d_ff // tf),
        in_specs=[pl.BlockSpec((tm, D_MODEL), lambda i, e, f: (i, 0)),
                  pl.BlockSpec((tm, PLE_DIM), lambda i, e, f: (i, 0)),
                  pl.BlockSpec((None, D_MODEL, tf), lambda i, e, f: (e, 0, f)),
                  pl.BlockSpec((None, D_MODEL, tf), lambda i, e, f: (e, 0, f)),
                  pl.BlockSpec((None, tf, D_MODEL), lambda i, e, f: (e, f, 0)),
                  pl.BlockSpec((D_MODEL, LANES), lambda i, e, f: (0, 0)),
                  pl.BlockSpec((D_MODEL, D_MODEL), lambda i, e, f: (0, 0)),
                  pl.BlockSpec((PLE_DIM, D_MODEL), lambda i, e, f: (0, 0)),
                  vec, vec],
        out_specs=pl.BlockSpec((tm, D_MODEL), lambda i, e, f: (i, 0)),
        out_shape=_sds((m, D_MODEL), F32),
        scratch_shapes=[pltpu.VMEM((tm, D_MODEL), BF16),
                        pltpu.VMEM((tm, D_MODEL), F32),
                        pltpu.VMEM((tm, LANES), F32)],
        compiler_params=_cparams(("arbitrary", "arbitrary", "arbitrary")),
        name="ffn_moe" if moe else "ffn_dense",
    )(x, p, w1_b, w3_b, w2_b, rw, pg_b, pp_b, g, b)


def _row_tile(m, pref):
    return pref if m % pref == 0 else m


def _trunk(x, p, conv_state, h_state, bsz, seq, w, attend):
    depth = w['w_in'].shape[0]
    m = bsz * seq
    alpha = (2.0 * depth) ** 0.25
    tm = _row_tile(m, 512)
    tt = _row_tile(seq, 256)
    xf = x.reshape(m, D_MODEL)
    ks, vs, hs, convs = [], [], [], []
    for l in range(depth):
        lam_init = 0.8 - 0.6 * math.exp(-0.3 * l)
        z, zb = _inproj(xf, w['w_in'][l], _row_tile(m, 2 * tm))
        y_rnn, h_last = _rnn(z, conv_state[l], h_state[l].reshape(bsz, 1, D_RNN), w['conv_w'][l],
                             w['conv_b'][l].reshape(1, D_RNN), w['lru_wr'][l],
                             w['lru_br'][l].reshape(1, D_RNN), w['lru_wi'][l],
                             w['lru_bi'][l].reshape(1, D_RNN), w['lru_lambda'][l].reshape(1, D_RNN),
                             bsz, seq, tt)
        y_att = attend(l, z, zb, w['lamp'][l], w['subln_g'][l], lam_init)
        x1 = _merge(y_rnn, y_att, z, xf, w['w_rnn_out'][l], w['w_attn_out'][l], w['w_o'][l],
                    w['ln1_g'][l].reshape(1, D_MODEL), w['ln1_b'][l].reshape(1, D_MODEL), tm, alpha)
        mi = l // 2
        if l % 2 == 0:
            w1, w3, w2 = (w['ffn_w1'][mi][None], w['ffn_w3'][mi][None], w['ffn_w2'][mi][None])
            rw = w['router_pad'][0]
        else:
            w1, w3, w2 = w['moe_w1'][mi], w['moe_w3'][mi], w['moe_w2'][mi]
            rw = w['router_pad'][mi]
        xf = _ffn(x1, p[l].reshape(m, PLE_DIM), w1, w3, w2, rw, w['ple_gate_w'][l], w['ple_proj_w'][l],
                  w['ln2_g'][l].reshape(1, D_MODEL), w['ln2_b'][l].reshape(1, D_MODEL),
                  tm, w1.shape[2] // 2, alpha, moe=(l % 2 == 1))
        ks.append(z[3].reshape(bsz, seq, ATT_HEADS, ATT_VDIM))
        vs.append(z[4].reshape(bsz, seq, ATT_HEADS, ATT_VDIM))
        hs.append(h_last.reshape(bsz, D_RNN))
        xr = z[0].reshape(bsz, seq, D_RNN)
        xpad = jnp.concatenate([conv_state[l].astype(xr.dtype), xr], axis=1)
        convs.append(xpad[:, -(CONV_W - 1):])
    return (xf.reshape(bsz, seq, D_MODEL), jnp.stack(ks), jnp.stack(vs), jnp.stack(hs), jnp.stack(convs))


def kernel(x_prompt, x_sample, cache_k, cache_v, state_h, state_conv, page_table, p_prompt, p_sample, rel_bias, w_in, conv_w, conv_b, lru_wr, lru_br, lru_wi, lru_bi, lru_lambda, w_rnn_out, w_attn_out, lam_q1, lam_k1, lam_q2, lam_k2, subln_g, w_o, ln1_g, ln1_b, ffn_w1, ffn_w3, ffn_w2, router_w, moe_w1, moe_w3, moe_w2, ple_gate_w, ple_proj_w, ln2_g, ln2_b):
    depth = w_in.shape[0]
    bsz, seq, _ = x_prompt.shape
    dbsz, dseq, _ = x_sample.shape
    n_pool, page_size = cache_k.shape[1], cache_k.shape[2]
    router_pad = jnp.zeros((max(router_w.shape[0], 1), D_MODEL, LANES), F32)
    if router_w.shape[0]:
        router_pad = router_pad.at[:, :, :N_EXPERTS].set(router_w.astype(F32))
    w = dict(
        w_in=w_in.astype(BF16), conv_w=conv_w, conv_b=conv_b,
        lru_wr=jax.vmap(_block_diag_groups)(lru_wr).astype(BF16), lru_br=lru_br,
        lru_wi=jax.vmap(_block_diag_groups)(lru_wi).astype(BF16), lru_bi=lru_bi,
        lru_lambda=lru_lambda, w_rnn_out=w_rnn_out.astype(BF16), w_attn_out=w_attn_out.astype(BF16),
        lamp=jnp.stack([lam_q1, lam_k1, lam_q2, lam_k2], axis=1).astype(F32), subln_g=subln_g.astype(F32),
        w_o=w_o.astype(BF16), ln1_g=ln1_g, ln1_b=ln1_b,
        ffn_w1=ffn_w1.astype(BF16), ffn_w3=ffn_w3.astype(BF16), ffn_w2=ffn_w2.astype(BF16),
        router_pad=router_pad, moe_w1=moe_w1.astype(BF16), moe_w3=moe_w3.astype(BF16),
        moe_w2=moe_w2.astype(BF16), ple_gate_w=ple_gate_w.astype(BF16),
        ple_proj_w=ple_proj_w.astype(BF16), ln2_g=ln2_g, ln2_b=ln2_b)

    tq = _row_tile(seq, 256)
    assert tq >= MAX_DIST and page_size >= MAX_DIST
    pbias = _prompt_bias(rel_bias, tq)

    def prompt_attend(l, z, zb, lamp, g, lam_init):
        del l, z
        return _prompt_attend(zb, pbias, lamp, g.reshape(ATT_VDIM, 1), bsz, seq, tq, lam_init)

    conv0 = jnp.zeros((depth, bsz, CONV_W - 1, D_RNN), x_prompt.dtype)
    h0 = jnp.zeros((depth, bsz, D_RNN), F32)
    y_prompt, k_prompt, v_prompt, h_prompt, conv_prompt = _trunk(
        x_prompt, p_prompt, conv0, h0, bsz, seq, w, prompt_attend)

    rows = page_size * ATT_HEADS
    cache_k4 = cache_k.reshape(depth, n_pool, rows, ATT_VDIM)
    cache_v4 = cache_v.reshape(depth, n_pool, rows, ATT_VDIM)
    mask, blast, bnew = _sample_tables(rel_bias, page_size, dseq)
    n_pages = page_table.shape[1]
    n_pp = 8 if n_pages % 8 == 0 else (4 if n_pages % 4 == 0 else 2)

    def sample_attend(l, z, zb, lamp, g, lam_init):
        del zb
        wq = _sample_wq(z[2], dbsz, dseq)
        knew = z[3].reshape(dbsz, dseq * ATT_HEADS, ATT_VDIM)
        vnew = z[4].reshape(dbsz, dseq * ATT_HEADS, ATT_VDIM)
        o = _sample_attend(l, page_table, cache_k4, cache_v4, wq, knew, vnew, blast, bnew, mask,
                           lamp, g.reshape(ATT_VDIM, 1), n_pp, lam_init)
        o = o.reshape(dbsz, ATT_VDIM, ATT_HEADS, dseq)
        return jnp.transpose(o, (0, 3, 2, 1)).reshape(dbsz * dseq, ATT_HEADS * ATT_VDIM).astype(BF16)

    y_sample, k_sample, v_sample, h_sample, conv_sample = _trunk(
        x_sample, p_sample, state_conv, state_h, dbsz, dseq, w, sample_attend)
    return (y_prompt, y_sample, k_prompt, v_prompt, h_prompt, conv_prompt,
            k_sample, v_sample, h_sample, conv_sample)
```

```python
import functools
import math

import jax
import jax.numpy as jnp
from jax import lax
from jax.experimental import pallas as pl
from jax.experimental.pallas import tpu as pltpu

F32 = jnp.float32
BF16 = jnp.bfloat16

D_MODEL = 1024
D_RNN = 1024
RNN_HEADS = 16
RNN_BLOCK = D_RNN // RNN_HEADS
CONV_W = 4
LRU_C = 8.0
ATT_HEADS = 8
ATT_DH = 64
ATT_VDIM = 2 * ATT_DH
N_PLANES = 7
N_BUCKETS = 32
MAX_EXACT = N_BUCKETS // 2
MAX_DIST = 128
N_EXPERTS = 8
PLE_DIM = 256
LN_EPS = 1e-5
NEG_INF = -1e30
QK_SCALE = ATT_DH ** -0.5
LOG2E = math.log2(math.e)
VT_PAD = 16
PROMPT_HEADS_PER_STEP = 2

LANES = 128
SUBLANES = 8
MXU_DIM = 256
VMEM_LIMIT = 56 << 20


def _cparams(sem):
    return pltpu.CompilerParams(dimension_semantics=sem, vmem_limit_bytes=VMEM_LIMIT)


def _sds(shape, dtype):
    return jax.ShapeDtypeStruct(shape, dtype)


def _layer_norm_rows(y, g, b):
    mu = jnp.mean(y, axis=-1, keepdims=True)
    yc = y - mu
    var = jnp.mean(yc * yc, axis=-1, keepdims=True)
    return yc * lax.rsqrt(var + LN_EPS) * g + b


def _inproj_body(x_ref, w_ref, z_ref, zb_ref, xb_scr):
    j = pl.program_id(1)

    @pl.when(j == 0)
    def _():
        xb_scr[...] = x_ref[...].astype(BF16)

    acc = jnp.dot(xb_scr[...], w_ref[...], preferred_element_type=F32)
    z_ref[...] = acc

    @pl.when(j == 2)
    def _():
        zb_ref[...] = (acc * (QK_SCALE * LOG2E)).astype(BF16)

    @pl.when(jnp.logical_or(j == 3, j == 4))
    def _():
        zb_ref[...] = acc.astype(BF16)


def _inproj(x, w_b, tm):
    m = x.shape[0]
    return pl.pallas_call(
        _inproj_body,
        grid=(m // tm, N_PLANES),
        in_specs=[pl.BlockSpec((tm, D_MODEL), lambda i, j: (i, 0)),
                  pl.BlockSpec((D_MODEL, D_MODEL), lambda i, j: (0, j))],
        out_specs=[pl.BlockSpec((None, tm, D_MODEL), lambda i, j: (j, i, 0)),
                   pl.BlockSpec((None, tm, D_MODEL), lambda i, j: (jnp.clip(j - 2, 0, 2), i, 0))],
        out_shape=[_sds((N_PLANES, m, D_MODEL), F32), _sds((3, m, D_MODEL), BF16)],
        scratch_shapes=[pltpu.VMEM((tm, D_MODEL), BF16)],
        compiler_params=_cparams(("arbitrary", "arbitrary")),
        name="inproj",
    )(x, w_b)


def _rnn_body(xr_ref, gr_ref, conv0_ref, h0_ref, cw_ref, cb_ref, wr_ref, br_ref, wi_ref, bi_ref,
              lam_ref, y_ref, hlast_ref, xpad_scr, a_scr, b_scr, h_scr, hs_scr):
    t = pl.program_id(1)
    tt = xr_ref.shape[0]
    halo = SUBLANES

    @pl.when(t == 0)
    def _():
        xpad_scr[halo - 3:halo, :] = conv0_ref[...]
        h_scr[...] = h0_ref[...]

    xr = xr_ref[...]
    xpad_scr[halo:halo + tt, :] = xr
    cw = cw_ref[...]
    xc = cb_ref[...] + xpad_scr[halo - 3:halo - 3 + tt, :] * cw[0:1]
    xc = xc + xpad_scr[halo - 2:halo - 2 + tt, :] * cw[1:2]
    xc = xc + xpad_scr[halo - 1:halo - 1 + tt, :] * cw[2:3]
    xc = xc + xr * cw[3:4]
    tail = xpad_scr[halo + tt - 3:halo + tt, :]
    xpad_scr[halo - 3:halo, :] = tail

    xcb = xc.astype(BF16)
    n_grp = D_RNN // MXU_DIM
    r_parts, i_parts = [], []
    for g in range(n_grp):
        xg = xcb[:, g * MXU_DIM:(g + 1) * MXU_DIM]
        r_parts.append(jnp.dot(xg, wr_ref[g], preferred_element_type=F32))
        i_parts.append(jnp.dot(xg, wi_ref[g], preferred_element_type=F32))
    r = jax.nn.sigmoid(jnp.concatenate(r_parts, axis=1) + br_ref[...])
    ig = jax.nn.sigmoid(jnp.concatenate(i_parts, axis=1) + bi_ref[...])
    log_a = -LRU_C * r * jax.nn.softplus(-lam_ref[...])
    a = jnp.exp(log_a)
    a_scr[...] = a
    b_scr[...] = jnp.sqrt(-jnp.tanh(log_a) * (a * a + 1.0)) * (ig * xc)

    def step(i, h):
        h = a_scr[pl.ds(i, 1), :] * h + b_scr[pl.ds(i, 1), :]
        hs_scr[pl.ds(i, 1), :] = h
        return h

    h = lax.fori_loop(0, tt, step, h_scr[...], unroll=8)
    h_scr[...] = h
    y_ref[...] = (jax.nn.gelu(gr_ref[...]) * hs_scr[...]).astype(BF16)

    @pl.when(t == pl.num_programs(1) - 1)
    def _():
        hlast_ref[...] = h


def _rnn(z, conv0, h0, cw, cb, wr_b, br, wi_b, bi, lam, bsz, seq, tt):
    m = bsz * seq
    nt = seq // tt
    n_grp = D_RNN // MXU_DIM
    row = lambda b, t: (0, 0)
    return pl.pallas_call(
        _rnn_body,
        grid=(bsz, nt),
        in_specs=[pl.BlockSpec((None, tt, D_RNN), lambda b, t: (0, b * nt + t, 0)),
                  pl.BlockSpec((None, tt, D_RNN), lambda b, t: (1, b * nt + t, 0)),
                  pl.BlockSpec((None, CONV_W - 1, D_RNN), lambda b, t: (b, 0, 0)),
                  pl.BlockSpec((None, 1, D_RNN), lambda b, t: (b, 0, 0)),
                  pl.BlockSpec((CONV_W, D_RNN), row),
                  pl.BlockSpec((1, D_RNN), row),
                  pl.BlockSpec((n_grp, MXU_DIM, MXU_DIM), lambda b, t: (0, 0, 0)),
                  pl.BlockSpec((1, D_RNN), row),
                  pl.BlockSpec((n_grp, MXU_DIM, MXU_DIM), lambda b, t: (0, 0, 0)),
                  pl.BlockSpec((1, D_RNN), row),
                  pl.BlockSpec((1, D_RNN), row)],
        out_specs=[pl.BlockSpec((tt, D_RNN), lambda b, t: (b * nt + t, 0)),
                   pl.BlockSpec((None, 1, D_RNN), lambda b, t: (b, 0, 0))],
        out_shape=[_sds((m, D_RNN), BF16), _sds((bsz, 1, D_RNN), F32)],
        scratch_shapes=[pltpu.VMEM((tt + SUBLANES, D_RNN), F32),
                        pltpu.VMEM((tt, D_RNN), F32),
                        pltpu.VMEM((tt, D_RNN), F32),
                        pltpu.VMEM((1, D_RNN), F32),
                        pltpu.VMEM((tt, D_RNN), F32)],
        compiler_params=_cparams(("arbitrary", "arbitrary")),
        name="rnn",
    )(z, z, conv0, h0, cw, cb, wr_b, br, wi_b, bi, lam)


def _block_diag_groups(w):
    per = MXU_DIM // RNN_BLOCK
    n_grp = RNN_HEADS // per
    wg = w.reshape(n_grp, per, RNN_BLOCK, RNN_BLOCK)
    eye = jnp.eye(per, dtype=w.dtype)
    out = jnp.einsum('gpij,pq->gpiqj', wg, eye)
    return out.reshape(n_grp, MXU_DIM, MXU_DIM)


def _bucket(n):
    n_f = jnp.maximum(n, 1).astype(F32)
    large = MAX_EXACT + (jnp.log(n_f / MAX_EXACT) / math.log(MAX_DIST / MAX_EXACT)
                         * (N_BUCKETS - MAX_EXACT)).astype(jnp.int32)
    return jnp.where(n < MAX_EXACT, n, jnp.minimum(large, N_BUCKETS - 1))


def _lambda_value(lamp_ref, lam_init):
    lp = lamp_ref[...]
    s1 = jnp.sum(lp[0:1] * lp[1:2], axis=-1, keepdims=True)
    s2 = jnp.sum(lp[2:3] * lp[3:4], axis=-1, keepdims=True)
    return jnp.exp(s1) - jnp.exp(s2) + lam_init


def _table_lookup(tab, bucket):
    ids = jnp.arange(N_BUCKETS, dtype=bucket.dtype)[None, :, None, None]
    return jnp.sum(jnp.where(bucket[None, None] == ids, tab.T[:, :, None, None], 0.0), axis=1)


def _prompt_bias(rel_bias, tq):
    j = jnp.arange(tq)[:, None]
    i = jnp.arange(tq)[None, :]
    tab = rel_bias.astype(F32)
    tab = tab - tab[N_BUCKETS - 1]
    prev = _table_lookup(tab, _bucket(tq + i - j))
    diag = _table_lookup(tab, _bucket(jnp.maximum(i - j, 0)))
    diag = jnp.where((j <= i)[None], diag * LOG2E, NEG_INF)
    b4 = jnp.stack([jnp.zeros_like(prev), prev * LOG2E, diag, jnp.full_like(prev, NEG_INF)], axis=1)
    return jnp.concatenate([b4, b4], axis=3)


def _pattn_body(q_ref, k_ref, v_ref, bias_ref, lamp_ref, g_ref, o_ref, vt_scr, s_scr, acc_scr,
                *, lam_init, n_hd):
    qi = pl.program_id(2)
    tq = q_ref.shape[0]
    tk = vt_scr.shape[-1]
    n_blk = k_ref.shape[0] // tk

    @pl.when(qi == 0)
    def _():
        def transpose_values(c, carry):
            start = pl.multiple_of(c * tk, tk)
            for hh in range(n_hd):
                vb = v_ref[pl.ds(start, tk), hh * ATT_VDIM:(hh + 1) * ATT_VDIM].astype(F32)
                vt_scr[hh, c, :ATT_VDIM, :] = vb.T.astype(BF16)
                vt_scr[hh, c, ATT_VDIM:, :] = jnp.ones((VT_PAD, tk), BF16)
            return carry

        lax.fori_loop(0, n_blk, transpose_values, 0)

    row = lax.broadcasted_iota(jnp.int32, (ATT_VDIM, tq), 0)
    qqt = []
    for hh in range(n_hd):
        qt = q_ref[:, hh * ATT_VDIM:(hh + 1) * ATT_VDIM].astype(F32).T
        qqt.append(jnp.concatenate([jnp.where(row < ATT_DH, qt, 0.0), jnp.where(row >= ATT_DH, qt, 0.0)],
                                   axis=1).astype(BF16))

    def scores(hh, ki):
        start = pl.multiple_of(jnp.minimum(ki, qi) * tk, tk)
        kb = k_ref[pl.ds(start, tk), hh * ATT_VDIM:(hh + 1) * ATT_VDIM]
        return jnp.dot(kb, qqt[hh], preferred_element_type=F32)

    def consume(hh, slot, ki, m):
        sel = jnp.clip(ki - (qi - 2), 0, 3)
        st = s_scr[hh, slot] + bias_ref[hh, sel]
        m_new = jnp.maximum(m, jnp.max(st, axis=0, keepdims=True))
        alpha = jnp.exp2(m - m_new)
        p = jnp.exp2(st - m_new)
        pv = jnp.dot(vt_scr[hh, jnp.minimum(ki, qi)], p.astype(BF16), preferred_element_type=F32)
        acc_scr[hh] = alpha * acc_scr[hh] + pv
        return m_new

    def pair_step(pi, stats):
        k0 = 2 * pi
        stats = list(stats)
        for hh in range(n_hd):
            s_scr[hh, 1] = scores(hh, k0 + 1)
        for hh in range(n_hd):
            stats[hh] = consume(hh, 0, k0, stats[hh])
        for hh in range(n_hd):
            s_scr[hh, 0] = scores(hh, k0 + 2)
        for hh in range(n_hd):
            stats[hh] = consume(hh, 1, k0 + 1, stats[hh])
        return tuple(stats)

    for hh in range(n_hd):
        s_scr[hh, 0] = scores(hh, 0)
        acc_scr[hh] = jnp.zeros(acc_scr.shape[1:], F32)
    init = tuple(jnp.full((1, 2 * tq), NEG_INF, F32) for _ in range(n_hd))
    lax.fori_loop(0, lax.shift_right_logical(qi + 2, 1), pair_step, init)
    lam = _lambda_value(lamp_ref, lam_init)
    for hh in range(n_hd):
        acc = acc_scr[hh]
        a = acc[:ATT_VDIM] / acc[ATT_VDIM:ATT_VDIM + 1]
        o = a[:, :tq] - lam * a[:, tq:]
        o = o * lax.rsqrt(jnp.mean(o * o, axis=0, keepdims=True) + LN_EPS) * g_ref[...] * (1.0 - lam_init)
        o_ref[:, hh * ATT_VDIM:(hh + 1) * ATT_VDIM] = o.T.astype(BF16)


def _prompt_attend(zb, bias, lamp, g_col, bsz, seq, tq, lam_init):
    m = bsz * seq
    nq = seq // tq
    n_hd = PROMPT_HEADS_PER_STEP
    wd = n_hd * ATT_VDIM
    return pl.pallas_call(
        functools.partial(_pattn_body, lam_init=lam_init, n_hd=n_hd),
        grid=(bsz, ATT_HEADS // n_hd, nq),
        in_specs=[pl.BlockSpec((None, tq, wd), lambda b, h, i: (0, b * nq + i, h)),
                  pl.BlockSpec((None, seq, wd), lambda b, h, i: (1, b, h)),
                  pl.BlockSpec((None, seq, wd), lambda b, h, i: (2, b, h)),
                  pl.BlockSpec((n_hd, 4, tq, 2 * tq), lambda b, h, i: (h, 0, 0, 0)),
                  pl.BlockSpec((4, ATT_DH), lambda b, h, i: (0, 0)),
                  pl.BlockSpec((ATT_VDIM, 1), lambda b, h, i: (0, 0))],
        out_specs=pl.BlockSpec((tq, wd), lambda b, h, i: (b * nq + i, h)),
        out_shape=_sds((m, ATT_HEADS * ATT_VDIM), BF16),
        scratch_shapes=[pltpu.VMEM((n_hd, seq // tq, ATT_VDIM + VT_PAD, tq), BF16),
                        pltpu.VMEM((n_hd, 2, tq, 2 * tq), F32),
                        pltpu.VMEM((n_hd, ATT_VDIM + VT_PAD, 2 * tq), F32)],
        compiler_params=_cparams(("arbitrary", "arbitrary", "arbitrary")),
        name="prompt_attn",
    )(zb, zb, zb, bias, lamp, g_col)


def _sattn_body(pt_ref, *refs, n_pp, lam_init):
    del pt_ref
    kp_refs = refs[:n_pp]
    vp_refs = refs[n_pp:2 * n_pp]
    (wq_ref, knew_ref, vnew_ref, blast_ref, bnew_ref, mask_ref, lamp_ref, g_ref,
     o_ref, m_scr, l_scr, acc_scr, s_scr) = refs[2 * n_pp:]
    c = pl.program_id(1)
    last = c == pl.num_programs(1) - 1
    rows = kp_refs[0].shape[0]
    n_tok = rows // ATT_HEADS

    @pl.when(c == 0)
    def _():
        m_scr[...] = jnp.full(m_scr.shape, NEG_INF, F32)
        l_scr[...] = jnp.zeros(l_scr.shape, F32)
        acc_scr[...] = jnp.zeros(acc_scr.shape, F32)

    valid = mask_ref[...] > 0.0

    def update(state, s3, vs):
        m_old, l, acc = state
        n_grp = len(vs)
        m2 = jnp.max(s3, axis=0)
        mx = m2[:, :LANES]
        for g in range(1, n_grp):
            mx = jnp.maximum(mx, m2[:, g * LANES:(g + 1) * LANES])
        m_new = jnp.maximum(m_old, mx)
        alpha = jnp.exp2(m_old - m_new)
        m_sub = jnp.where(valid, m_new, -NEG_INF)
        p = jnp.exp2(s3 - jnp.concatenate([m_sub] * n_grp, axis=1)[None])
        l2 = jnp.sum(p, axis=0)
        p2 = p.reshape(vs[0].shape[0], n_grp * LANES).astype(BF16)
        l = alpha * l
        acc = acc * jnp.sum(jnp.where(valid, alpha, 0.0), axis=0, keepdims=True)
        for g, vv in enumerate(vs):
            l = l + l2[:, g * LANES:(g + 1) * LANES]
            acc = acc + lax.dot_general(vv, p2[:, g * LANES:(g + 1) * LANES], (((0,), (0,)), ((), ())),
                                        preferred_element_type=F32)
        return m_new, l, acc

    wq2 = wq_ref[...]

    def pair_scores(pp):
        k2 = jnp.concatenate([kp_refs[2 * pp][...].astype(BF16), kp_refs[2 * pp + 1][...].astype(BF16)], axis=1)
        return jnp.dot(k2, wq2, preferred_element_type=F32)

    n_pairs = n_pp // 2
    state = (m_scr[...], l_scr[...], acc_scr[...])
    s_scr[0] = pair_scores(0)
    for pp in range(n_pairs):
        if pp + 1 < n_pairs:
            s_scr[(pp + 1) % 2] = pair_scores(pp + 1)
        s3 = s_scr[pp % 2].reshape(n_tok, ATT_HEADS, 2 * LANES)
        if pp == n_pairs - 1:
            s3 = s3 + jnp.where(last, blast_ref[...].reshape(n_tok, ATT_HEADS, 2 * LANES), 0.0)
        state = update(state, s3, [vp_refs[2 * pp][...].astype(BF16), vp_refs[2 * pp + 1][...].astype(BF16)])
    m_scr[...], l_scr[...], acc_scr[...] = state

    @pl.when(last)
    def _():
        n_new = knew_ref.shape[0] // ATT_HEADS
        sn = jnp.dot(knew_ref[...].astype(BF16), wq_ref[:LANES, :LANES], preferred_element_type=F32)
        sn = sn + bnew_ref[...]
        _, l, acc = update(state, sn.reshape(n_new, ATT_HEADS, LANES), [vnew_ref[...].astype(BF16)])
        l_row = jnp.sum(jnp.where(valid, l, 0.0), axis=0, keepdims=True)
        a = acc / l_row
        lam = _lambda_value(lamp_ref, lam_init)
        half = LANES // 2
        o = a[:, :half] - lam * a[:, half:]
        o = o * lax.rsqrt(jnp.mean(o * o, axis=0, keepdims=True) + LN_EPS) * g_ref[...] * (1.0 - lam_init)
        o_ref[...] = o


def _sample_attend(layer, page_table, cache_k4, cache_v4, wq2, knew, vnew, blast, bnew, mask,
                   lamp, g_col, n_pp, lam_init):
    bsz, n_pages = page_table.shape
    rows = cache_k4.shape[2]
    n_new = knew.shape[1]
    nc = n_pages // n_pp

    def page_spec(p):
        return pl.BlockSpec((None, None, rows, LANES),
                            lambda b, c, pt: (layer, pt[b, c * n_pp + p], 0, 0))

    const2 = lambda b, c, pt: (0, 0)
    in_specs = ([page_spec(p) for p in range(n_pp)] + [page_spec(p) for p in range(n_pp)] + [
        pl.BlockSpec((None, 2 * LANES, 2 * LANES), lambda b, c, pt: (b, 0, 0)),
        pl.BlockSpec((None, n_new, LANES), lambda b, c, pt: (b, 0, 0)),
        pl.BlockSpec((None, n_new, LANES), lambda b, c, pt: (b, 0, 0)),
        pl.BlockSpec((rows, 2 * LANES), const2),
        pl.BlockSpec((n_new, LANES), const2),
        pl.BlockSpec((SUBLANES, LANES), const2),
        pl.BlockSpec((4, ATT_DH), const2),
        pl.BlockSpec((ATT_VDIM, 1), const2)])
    grid_spec = pltpu.PrefetchScalarGridSpec(
        num_scalar_prefetch=1,
        grid=(bsz, nc),
        in_specs=in_specs,
        out_specs=pl.BlockSpec((None, ATT_VDIM, LANES // 2), lambda b, c, pt: (b, 0, 0)),
        scratch_shapes=[pltpu.VMEM((SUBLANES, LANES), F32),
                        pltpu.VMEM((SUBLANES, LANES), F32),
                        pltpu.VMEM((ATT_VDIM, LANES), F32),
                        pltpu.VMEM((2, rows, 2 * LANES), F32)])
    return pl.pallas_call(
        functools.partial(_sattn_body, n_pp=n_pp, lam_init=lam_init),
        grid_spec=grid_spec,
        out_shape=_sds((bsz, ATT_VDIM, LANES // 2), F32),
        compiler_params=_cparams(("arbitrary", "arbitrary")),
        name="sample_attn",
    )(page_table, *([cache_k4] * n_pp), *([cache_v4] * n_pp), wq2, knew, vnew, blast, bnew, mask,
      lamp, g_col)


def _sample_tables(rel_bias, page_size, dec_seq):
    tab = rel_bias.astype(F32)
    tab = (tab - tab[N_BUCKETS - 1]) * LOG2E
    col = jnp.arange(LANES)
    col_h = (col % (LANES // 2)) // dec_seq
    col_q = col % dec_seq
    row_h = jnp.arange(ATT_HEADS)
    mask = (row_h[:, None] == col_h[None, :]).astype(F32)
    t = jnp.arange(page_size)
    dist = page_size + col_q[None, :] - t[:, None]
    blast = _table_lookup(tab, _bucket(dist))
    blast = jnp.moveaxis(blast, 0, 1).reshape(page_size * ATT_HEADS, LANES)
    blast = jnp.concatenate([jnp.zeros_like(blast), blast], axis=1)
    j = jnp.arange(dec_seq)
    dn = col_q[None, :] - j[:, None]
    bnew = jnp.where((dn >= 0)[:, None, :],
                     jnp.moveaxis(_table_lookup(tab, _bucket(jnp.maximum(dn, 0))), 0, 1), NEG_INF)
    bnew = bnew.reshape(dec_seq * ATT_HEADS, LANES)
    return mask, blast, bnew


def _sample_wq(q_plane, bsz, dec_seq):
    q = q_plane.reshape(bsz, dec_seq, ATT_HEADS, 2, ATT_DH) * (QK_SCALE * LOG2E)
    eye = jnp.eye(2, dtype=q.dtype)
    w = jnp.einsum('bqhcd,ce->bcdehq', q, eye).reshape(bsz, LANES, LANES)
    w2 = jnp.einsum('bfc,pq->bpfqc', w, eye)
    return w2.reshape(bsz, 2 * LANES, 2 * LANES).astype(BF16)


def _top2_combine(x, rw_ref):
    logits = jnp.dot(x, rw_ref[...], preferred_element_type=F32, precision=lax.Precision.HIGHEST)
    lane = lax.broadcasted_iota(jnp.int32, logits.shape, 1).astype(F32)
    big = float(LANES)
    lg = jnp.where(lane < N_EXPERTS, logits, -jnp.inf)
    m1 = jnp.max(lg, axis=1, keepdims=True)
    i1 = jnp.min(jnp.where(lg == m1, lane, big), axis=1, keepdims=True)
    lg2 = jnp.where(lane == i1, -jnp.inf, lg)
    m2 = jnp.max(lg2, axis=1, keepdims=True)
    i2 = jnp.min(jnp.where(lg2 == m2, lane, big), axis=1, keepdims=True)
    e2 = jnp.exp(m2 - m1)
    den = 1.0 + e2
    return jnp.where(lane == i1, 1.0 / den, 0.0) + jnp.where(lane == i2, e2 / den, 0.0)


def _merge_body(yr_ref, ya_ref, ga_ref, gb_ref, x_ref, wr_ref, wa_ref, wo_ref, g_ref, b_ref, *rest,
                alpha, route):
    pr = jnp.dot(yr_ref[...], wr_ref[...], preferred_element_type=F32)
    pa = jnp.dot(ya_ref[...], wa_ref[...], preferred_element_type=F32)
    merged = jax.nn.sigmoid(ga_ref[...]) * pr + jax.nn.sigmoid(gb_ref[...]) * pa
    mix = jnp.dot(merged.astype(BF16), wo_ref[...], preferred_element_type=F32)
    x1 = _layer_norm_rows(alpha * x_ref[...] + mix, g_ref[...], b_ref[...])
    if route:
        rw_ref, o_ref, xb_ref, comb_ref = rest
        xb_ref[...] = x1.astype(BF16)
        comb_ref[...] = _top2_combine(x1, rw_ref)
    else:
        (o_ref,) = rest
    o_ref[...] = x1


def _merge(yr, ya, z, x, wr_b, wa_b, wo_b, g, b, tm, alpha, rw=None):
    m = x.shape[0]
    route = rw is not None
    rows = pl.BlockSpec((tm, D_MODEL), lambda i: (i, 0))
    full = pl.BlockSpec((D_MODEL, D_MODEL), lambda i: (0, 0))
    vec = pl.BlockSpec((1, D_MODEL), lambda i: (0, 0))
    in_specs = [rows, rows,
                pl.BlockSpec((None, tm, D_MODEL), lambda i: (5, i, 0)),
                pl.BlockSpec((None, tm, D_MODEL), lambda i: (6, i, 0)),
                rows, full, full, full, vec, vec]
    out_specs, out_shape, args = [rows], [_sds((m, D_MODEL), F32)], [yr, ya, z, z, x, wr_b, wa_b, wo_b, g, b]
    if route:
        in_specs.append(pl.BlockSpec((D_MODEL, LANES), lambda i: (0, 0)))
        args.append(rw)
        out_specs += [rows, pl.BlockSpec((tm, LANES), lambda i: (i, 0))]
        out_shape += [_sds((m, D_MODEL), BF16), _sds((m, LANES), F32)]
    return pl.pallas_call(
        functools.partial(_merge_body, alpha=alpha, route=route),
        grid=(m // tm,),
        in_specs=in_specs,
        out_specs=out_specs,
        out_shape=out_shape,
        compiler_params=_cparams(("arbitrary",)),
        name="merge_route" if route else "merge",
    )(*args)


def _ple_ln(x, ffn, p_ref, pg_ref, pp_ref, g_ref, b_ref, xb, alpha):
    gate = jax.nn.sigmoid(jnp.dot(xb, pg_ref[...], preferred_element_type=F32))
    ple = gate * jnp.dot(p_ref[...].astype(BF16), pp_ref[...], preferred_element_type=F32)
    return _layer_norm_rows(alpha * x + ffn + ple, g_ref[...], b_ref[...])


def _ffn_body(x_ref, p_ref, w1_ref, w3_ref, w2_ref, pg_ref, pp_ref, g_ref, b_ref, o_ref,
              xb_scr, acc_scr, *, alpha):
    f = pl.program_id(1)

    @pl.when(f == 0)
    def _():
        xb_scr[...] = x_ref[...].astype(BF16)
        acc_scr[...] = jnp.zeros(acc_scr.shape, F32)

    xb = xb_scr[...]
    h1 = jnp.dot(xb, w1_ref[...], preferred_element_type=F32)
    h3 = jnp.dot(xb, w3_ref[...], preferred_element_type=F32)
    hh = (jax.nn.silu(h1) * h3).astype(BF16)
    acc_scr[...] += jnp.dot(hh, w2_ref[...], preferred_element_type=F32)

    @pl.when(f == pl.num_programs(1) - 1)
    def _():
        o_ref[...] = _ple_ln(x_ref[...], acc_scr[...], p_ref, pg_ref, pp_ref, g_ref, b_ref, xb, alpha)


def _ffn(x, p, w1_b, w3_b, w2_b, pg_b, pp_b, g, b, tm, tf, alpha):
    m = x.shape[0]
    d_ff = w1_b.shape[1]
    vec = pl.BlockSpec((1, D_MODEL), lambda i, f: (0, 0))
    return pl.pallas_call(
        functools.partial(_ffn_body, alpha=alpha),
        grid=(m // tm, d_ff // tf),
        in_specs=[pl.BlockSpec((tm, D_MODEL), lambda i, f: (i, 0)),
                  pl.BlockSpec((tm, PLE_DIM), lambda i, f: (i, 0)),
                  pl.BlockSpec((D_MODEL, tf), lambda i, f: (0, f)),
                  pl.BlockSpec((D_MODEL, tf), lambda i, f: (0, f)),
                  pl.BlockSpec((tf, D_MODEL), lambda i, f: (f, 0)),
                  pl.BlockSpec((D_MODEL, D_MODEL), lambda i, f: (0, 0)),
                  pl.BlockSpec((PLE_DIM, D_MODEL), lambda i, f: (0, 0)),
                  vec, vec],
        out_specs=pl.BlockSpec((tm, D_MODEL), lambda i, f: (i, 0)),
        out_shape=_sds((m, D_MODEL), F32),
        scratch_shapes=[pltpu.VMEM((tm, D_MODEL), BF16),
                        pltpu.VMEM((tm, D_MODEL), F32)],
        compiler_params=_cparams(("arbitrary", "arbitrary")),
        name="ffn_dense",
    )(x, p, w1_b, w3_b, w2_b, pg_b, pp_b, g, b)


def _moe_capacity(tm):
    cap = tm / 4 + 4.6 * math.sqrt(tm * 3 / 16)
    return min(tm, -(-int(math.ceil(cap)) // 16) * 16)


def _moe_body(xb_ref, comb_ref, w1_ref, w3_ref, w2_ref, o_ref, pos_scr, post_scr, combt_scr, xs_scr, ys_scr,
              *, cap):
    e = pl.program_id(1)
    f = pl.program_id(2)
    tm = xb_ref.shape[0]

    @pl.when(jnp.logical_and(e == 0, f == 0))
    def _():
        comb = comb_ref[...]
        sel = jnp.where(comb > 0.0, 1.0, 0.0).astype(BF16)
        r = lax.broadcasted_iota(jnp.int32, (tm, tm), 0)
        c = lax.broadcasted_iota(jnp.int32, (tm, tm), 1)
        pos_scr[...] = jnp.dot(jnp.where(c < r, 1.0, 0.0).astype(BF16), sel, preferred_element_type=F32)
        post_scr[...] = lax.dot_general(sel, jnp.where(r < c, 1.0, 0.0).astype(BF16), (((0,), (0,)), ((), ())),
                                        preferred_element_type=F32)
        combt_scr[...] = comb.T
        o_ref[...] = jnp.zeros(o_ref.shape, F32)

    lane = lax.broadcasted_iota(jnp.int32, (tm, LANES), 1)
    gate = jnp.sum(jnp.where(lane == e, comb_ref[...], 0.0), axis=1, keepdims=True)
    pos = jnp.sum(jnp.where(lane == e, pos_scr[...], 0.0), axis=1, keepdims=True)
    gate_t = combt_scr[pl.ds(e, 1), :]
    pos_t = post_scr[pl.ds(e, 1), :]
    n_rows = jnp.sum(jnp.where(gate_t > 0.0, 1.0, 0.0)).astype(jnp.int32)
    n_chunks = lax.div(n_rows + (cap - 1), cap)

    def chunk(ci, carry):
        base = (ci * cap).astype(F32)
        slot_col = lax.broadcasted_iota(jnp.int32, (1, cap), 1).astype(F32) + base
        slot_row = lax.broadcasted_iota(jnp.int32, (cap, 1), 0).astype(F32) + base
        scatter = jnp.where(jnp.logical_and(gate > 0.0, pos == slot_col), 1.0, 0.0).astype(BF16)

        @pl.when(f == 0)
        def _():
            gather = jnp.where(jnp.logical_and(gate_t > 0.0, pos_t == slot_row), 1.0, 0.0).astype(BF16)
            xs_scr[ci] = jnp.dot(gather, xb_ref[...], preferred_element_type=F32).astype(BF16)
            ys_scr[ci] = jnp.zeros(ys_scr.shape[1:], F32)

        xs = xs_scr[ci]
        h1 = jnp.dot(xs, w1_ref[...], preferred_element_type=F32)
        h3 = jnp.dot(xs, w3_ref[...], preferred_element_type=F32)
        hh = (jax.nn.silu(h1) * h3).astype(BF16)
        ys_scr[ci] += jnp.dot(hh, w2_ref[...], preferred_element_type=F32)

        @pl.when(f == pl.num_programs(2) - 1)
        def _():
            o_ref[...] += gate * jnp.dot(scatter, ys_scr[ci].astype(BF16), preferred_element_type=F32)

        return carry

    lax.fori_loop(0, n_chunks, chunk, 0)


def _moe(xb, comb, w1_b, w3_b, w2_b, tm, tf):
    m = xb.shape[0]
    n_e, _, d_ff = w1_b.shape
    cap = _moe_capacity(tm)
    n_slots = -(-tm // cap)
    return pl.pallas_call(
        functools.partial(_moe_body, cap=cap),
        grid=(m // tm, n_e, d_ff // tf),
        in_specs=[pl.BlockSpec((tm, D_MODEL), lambda i, e, f: (i, 0)),
                  pl.BlockSpec((tm, LANES), lambda i, e, f: (i, 0)),
                  pl.BlockSpec((None, D_MODEL, tf), lambda i, e, f: (e, 0, f)),
                  pl.BlockSpec((None, D_MODEL, tf), lambda i, e, f: (e, 0, f)),
                  pl.BlockSpec((None, tf, D_MODEL), lambda i, e, f: (e, f, 0))],
        out_specs=pl.BlockSpec((tm, D_MODEL), lambda i, e, f: (i, 0)),
        out_shape=_sds((m, D_MODEL), F32),
        scratch_shapes=[pltpu.VMEM((tm, LANES), F32),
                        pltpu.VMEM((LANES, tm), F32),
                        pltpu.VMEM((LANES, tm), F32),
                        pltpu.VMEM((n_slots, cap, D_MODEL), BF16),
                        pltpu.VMEM((n_slots, cap, D_MODEL), F32)],
        compiler_params=_cparams(("arbitrary", "arbitrary", "arbitrary")),
        name="moe",
    )(xb, comb, w1_b, w3_b, w2_b)


def _post_body(x_ref, f_ref, xb_ref, p_ref, pg_ref, pp_ref, g_ref, b_ref, o_ref, *, alpha):
    o_ref[...] = _ple_ln(x_ref[...], f_ref[...], p_ref, pg_ref, pp_ref, g_ref, b_ref, xb_ref[...], alpha)


def _post(x, ffn, xb, p, pg_b, pp_b, g, b, tm, alpha):
    m = x.shape[0]
    rows = pl.BlockSpec((tm, D_MODEL), lambda i: (i, 0))
    vec = pl.BlockSpec((1, D_MODEL), lambda i: (0, 0))
    return pl.pallas_call(
        functools.partial(_post_body, alpha=alpha),
        grid=(m // tm,),
        in_specs=[rows, rows, rows,
                  pl.BlockSpec((tm, PLE_DIM), lambda i: (i, 0)),
                  pl.BlockSpec((D_MODEL, D_MODEL), lambda i: (0, 0)),
                  pl.BlockSpec((PLE_DIM, D_MODEL), lambda i: (0, 0)),
                  vec, vec],
        out_specs=rows,
        out_shape=_sds((m, D_MODEL), F32),
        compiler_params=_cparams(("arbitrary",)),
        name="post",
    )(x, ffn, xb, p, pg_b, pp_b, g, b)


def _row_tile(m, pref):
    return pref if m % pref == 0 else m


def _trunk(x, p, conv_state, h_state, bsz, seq, w, attend):
    depth = w['w_in'].shape[0]
    m = bsz * seq
    alpha = (2.0 * depth) ** 0.25
    tm = _row_tile(m, 512)
    tt = _row_tile(seq, 256)
    xf = x.reshape(m, D_MODEL)
    ks, vs, hs, convs = [], [], [], []
    for l in range(depth):
        lam_init = 0.8 - 0.6 * math.exp(-0.3 * l)
        z, zb = _inproj(xf, w['w_in'][l], _row_tile(m, 2 * tm))
        y_rnn, h_last = _rnn(z, conv_state[l], h_state[l].reshape(bsz, 1, D_RNN), w['conv_w'][l],
                             w['conv_b'][l].reshape(1, D_RNN), w['lru_wr'][l],
                             w['lru_br'][l].reshape(1, D_RNN), w['lru_wi'][l],
                             w['lru_bi'][l].reshape(1, D_RNN), w['lru_lambda'][l].reshape(1, D_RNN),
                             bsz, seq, tt)
        y_att = attend(l, z, zb, w['lamp'][l], w['subln_g'][l], lam_init)
        mi = l // 2
        moe = l % 2 == 1
        merged = _merge(y_rnn, y_att, z, xf, w['w_rnn_out'][l], w['w_attn_out'][l], w['w_o'][l],
                        w['ln1_g'][l].reshape(1, D_MODEL), w['ln1_b'][l].reshape(1, D_MODEL), tm, alpha,
                        rw=w['router_pad'][mi] if moe else None)
        pl_ = p[l].reshape(m, PLE_DIM)
        ln2 = (w['ln2_g'][l].reshape(1, D_MODEL), w['ln2_b'][l].reshape(1, D_MODEL))
        if moe:
            x1, x1b, comb = merged
            tmoe = _row_tile(m, 2 * tm)
            ffn = _moe(x1b, comb, w['moe_w1'][mi], w['moe_w3'][mi], w['moe_w2'][mi], tmoe,
                       w['moe_w1'].shape[3] // 2)
            xf = _post(x1, ffn, x1b, pl_, w['ple_gate_w'][l], w['ple_proj_w'][l], *ln2, tm, alpha)
        else:
            (x1,) = merged
            xf = _ffn(x1, pl_, w['ffn_w1'][mi], w['ffn_w3'][mi], w['ffn_w2'][mi], w['ple_gate_w'][l],
                      w['ple_proj_w'][l], *ln2, tm, w['ffn_w1'].shape[2] // 2, alpha)
        ks.append(z[3].reshape(bsz, seq, ATT_HEADS, ATT_VDIM))
        vs.append(z[4].reshape(bsz, seq, ATT_HEADS, ATT_VDIM))
        hs.append(h_last.reshape(bsz, D_RNN))
        xr = z[0].reshape(bsz, seq, D_RNN)
        xpad = jnp.concatenate([conv_state[l].astype(xr.dtype), xr], axis=1)
        convs.append(xpad[:, -(CONV_W - 1):])
    return (xf.reshape(bsz, seq, D_MODEL), jnp.stack(ks), jnp.stack(vs), jnp.stack(hs), jnp.stack(convs))


def kernel(x_prompt, x_sample, cache_k, cache_v, state_h, state_conv, page_table, p_prompt, p_sample, rel_bias, w_in, conv_w, conv_b, lru_wr, lru_br, lru_wi, lru_bi, lru_lambda, w_rnn_out, w_attn_out, lam_q1, lam_k1, lam_q2, lam_k2, subln_g, w_o, ln1_g, ln1_b, ffn_w1, ffn_w3, ffn_w2, router_w, moe_w1, moe_w3, moe_w2, ple_gate_w, ple_proj_w, ln2_g, ln2_b):
    depth = w_in.shape[0]
    bsz, seq, _ = x_prompt.shape
    dbsz, dseq, _ = x_sample.shape
    n_pool, page_size = cache_k.shape[1], cache_k.shape[2]
    router_pad = jnp.pad(router_w.astype(F32), ((0, 0), (0, 0), (0, LANES - N_EXPERTS)))
    w = dict(
        w_in=w_in.astype(BF16), conv_w=conv_w, conv_b=conv_b,
        lru_wr=jax.vmap(_block_diag_groups)(lru_wr).astype(BF16), lru_br=lru_br,
        lru_wi=jax.vmap(_block_diag_groups)(lru_wi).astype(BF16), lru_bi=lru_bi,
        lru_lambda=lru_lambda, w_rnn_out=w_rnn_out.astype(BF16), w_attn_out=w_attn_out.astype(BF16),
        lamp=jnp.stack([lam_q1, lam_k1, lam_q2, lam_k2], axis=1).astype(F32), subln_g=subln_g.astype(F32),
        w_o=w_o.astype(BF16), ln1_g=ln1_g, ln1_b=ln1_b,
        ffn_w1=ffn_w1.astype(BF16), ffn_w3=ffn_w3.astype(BF16), ffn_w2=ffn_w2.astype(BF16),
        router_pad=router_pad, moe_w1=moe_w1.astype(BF16), moe_w3=moe_w3.astype(BF16),
        moe_w2=moe_w2.astype(BF16), ple_gate_w=ple_gate_w.astype(BF16),
        ple_proj_w=ple_proj_w.astype(BF16), ln2_g=ln2_g, ln2_b=ln2_b)

    tq = _row_tile(seq, 256)
    assert tq >= MAX_DIST and page_size >= MAX_DIST
    pbias = _prompt_bias(rel_bias, tq)

    def prompt_attend(l, z, zb, lamp, g, lam_init):
        del l, z
        return _prompt_attend(zb, pbias, lamp, g.reshape(ATT_VDIM, 1), bsz, seq, tq, lam_init)

    conv0 = jnp.zeros((depth, bsz, CONV_W - 1, D_RNN), x_prompt.dtype)
    h0 = jnp.zeros((depth, bsz, D_RNN), F32)
    y_prompt, k_prompt, v_prompt, h_prompt, conv_prompt = _trunk(
        x_prompt, p_prompt, conv0, h0, bsz, seq, w, prompt_attend)

    rows = page_size * ATT_HEADS
    cache_k4 = cache_k.reshape(depth, n_pool, rows, ATT_VDIM)
    cache_v4 = cache_v.reshape(depth, n_pool, rows, ATT_VDIM)
    mask, blast, bnew = _sample_tables(rel_bias, page_size, dseq)
    n_pages = page_table.shape[1]
    n_pp = next(n for n in (16, 8, 4, 2) if n_pages % n == 0)

    def sample_attend(l, z, zb, lamp, g, lam_init):
        del zb
        wq = _sample_wq(z[2], dbsz, dseq)
        knew = z[3].reshape(dbsz, dseq * ATT_HEADS, ATT_VDIM)
        vnew = z[4].reshape(dbsz, dseq * ATT_HEADS, ATT_VDIM)
        o = _sample_attend(l, page_table, cache_k4, cache_v4, wq, knew, vnew, blast, bnew, mask,
                           lamp, g.reshape(ATT_VDIM, 1), n_pp, lam_init)
        o = o.reshape(dbsz, ATT_VDIM, ATT_HEADS, dseq)
        return jnp.transpose(o, (0, 3, 2, 1)).reshape(dbsz * dseq, ATT_HEADS * ATT_VDIM).astype(BF16)

    y_sample, k_sample, v_sample, h_sample, conv_sample = _trunk(
        x_sample, p_sample, state_conv, state_h, dbsz, dseq, w, sample_attend)
    return (y_prompt, y_sample, k_prompt, v_prompt, h_prompt, conv_prompt,
            k_sample, v_sample, h_sample, conv_sample)
```

```python
import functools
import math

import jax
import jax.numpy as jnp
from jax import lax
from jax.experimental import pallas as pl
from jax.experimental.pallas import tpu as pltpu

F32 = jnp.float32
BF16 = jnp.bfloat16

D_MODEL = 1024
D_RNN = 1024
RNN_HEADS = 16
RNN_BLOCK = D_RNN // RNN_HEADS
CONV_W = 4
LRU_C = 8.0
ATT_HEADS = 8
ATT_DH = 64
ATT_VDIM = 2 * ATT_DH
N_PLANES = 7
N_BUCKETS = 32
MAX_EXACT = N_BUCKETS // 2
MAX_DIST = 128
N_EXPERTS = 8
PLE_DIM = 256
LN_EPS = 1e-5
NEG_INF = -1e30
QK_SCALE = ATT_DH ** -0.5
LOG2E = math.log2(math.e)
VT_PAD = 16
PROMPT_HEADS_PER_STEP = 2

LANES = 128
SUBLANES = 8
MXU_DIM = 256
VMEM_LIMIT = 56 << 20


def _cparams(sem):
    return pltpu.CompilerParams(dimension_semantics=sem, vmem_limit_bytes=VMEM_LIMIT)


def _sds(shape, dtype):
    return jax.ShapeDtypeStruct(shape, dtype)


def _layer_norm_rows(y, g, b):
    mu = jnp.mean(y, axis=-1, keepdims=True)
    yc = y - mu
    var = jnp.mean(yc * yc, axis=-1, keepdims=True)
    return yc * lax.rsqrt(var + LN_EPS) * g + b


def _inproj_body(x_ref, w_ref, *rest):
    z_ref, zb_ref, k_ref, v_ref, xb_scr = rest[-5:]
    j = pl.program_id(1)

    @pl.when(j == 0)
    def _():
        xb_scr[...] = x_ref[...].astype(BF16)

    acc = jnp.dot(xb_scr[...], w_ref[...], preferred_element_type=F32)

    @pl.when(jnp.logical_or(j < 3, j > 4))
    def _():
        z_ref[...] = acc

    @pl.when(j == 2)
    def _():
        zb_ref[...] = (acc * (QK_SCALE * LOG2E)).astype(BF16)

    @pl.when(j == 3)
    def _():
        k_ref[...] = acc
        zb_ref[...] = acc.astype(BF16)

    @pl.when(j == 4)
    def _():
        v_ref[...] = acc
        zb_ref[...] = acc.astype(BF16)


def _inproj(x, w_b, tm, layer, depth, kv_stack):
    m = x.shape[0]
    stacked = pl.BlockSpec((None, tm, D_MODEL), lambda i, j: (layer, i, 0))
    in_specs = [pl.BlockSpec((tm, D_MODEL), lambda i, j: (i, 0)),
                pl.BlockSpec((D_MODEL, D_MODEL), lambda i, j: (0, j))]
    if kv_stack is None:
        kv_stack = [jnp.zeros((depth, m, D_MODEL), F32)] * 2
    in_specs += [pl.BlockSpec(memory_space=pl.ANY)] * 2
    return pl.pallas_call(
        _inproj_body,
        grid=(m // tm, N_PLANES),
        in_specs=in_specs,
        out_specs=[pl.BlockSpec((None, tm, D_MODEL), lambda i, j: (jnp.where(j < 3, j, jnp.maximum(j - 2, 2)), i, 0)),
                   pl.BlockSpec((None, tm, D_MODEL), lambda i, j: (jnp.clip(j - 2, 0, 2), i, 0)),
                   stacked, stacked],
        out_shape=[_sds((N_PLANES - 2, m, D_MODEL), F32), _sds((3, m, D_MODEL), BF16),
                   _sds((depth, m, D_MODEL), F32), _sds((depth, m, D_MODEL), F32)],
        scratch_shapes=[pltpu.VMEM((tm, D_MODEL), BF16)],
        input_output_aliases={2: 2, 3: 3},
        compiler_params=_cparams(("arbitrary", "arbitrary")),
        name="inproj",
    )(x, w_b, *kv_stack)


def _rnn_body(xr_ref, gr_ref, conv0_ref, h0_ref, cw_ref, cb_ref, wr_ref, br_ref, wi_ref, bi_ref,
              lam_ref, y_ref, hlast_ref, xpad_scr, a_scr, b_scr, h_scr, hs_scr):
    t = pl.program_id(1)
    tt = xr_ref.shape[0]
    halo = SUBLANES

    @pl.when(t == 0)
    def _():
        xpad_scr[halo - 3:halo, :] = conv0_ref[...]
        h_scr[...] = h0_ref[...]

    xr = xr_ref[...]
    xpad_scr[halo:halo + tt, :] = xr
    cw = cw_ref[...]
    xc = cb_ref[...] + xpad_scr[halo - 3:halo - 3 + tt, :] * cw[0:1]
    xc = xc + xpad_scr[halo - 2:halo - 2 + tt, :] * cw[1:2]
    xc = xc + xpad_scr[halo - 1:halo - 1 + tt, :] * cw[2:3]
    xc = xc + xr * cw[3:4]
    tail = xpad_scr[halo + tt - 3:halo + tt, :]
    xpad_scr[halo - 3:halo, :] = tail

    xcb = xc.astype(BF16)
    n_grp = D_RNN // MXU_DIM
    r_parts, i_parts = [], []
    for g in range(n_grp):
        xg = xcb[:, g * MXU_DIM:(g + 1) * MXU_DIM]
        r_parts.append(jnp.dot(xg, wr_ref[g], preferred_element_type=F32))
        i_parts.append(jnp.dot(xg, wi_ref[g], preferred_element_type=F32))
    r = jax.nn.sigmoid(jnp.concatenate(r_parts, axis=1) + br_ref[...])
    ig = jax.nn.sigmoid(jnp.concatenate(i_parts, axis=1) + bi_ref[...])
    log_a = -LRU_C * r * jax.nn.softplus(-lam_ref[...])
    a = jnp.exp(log_a)
    a_scr[...] = a
    b_scr[...] = jnp.sqrt(-jnp.tanh(log_a) * (a * a + 1.0)) * (ig * xc)

    def step(i, h):
        h = a_scr[pl.ds(i, 1), :] * h + b_scr[pl.ds(i, 1), :]
        hs_scr[pl.ds(i, 1), :] = h
        return h

    h = lax.fori_loop(0, tt, step, h_scr[...], unroll=8)
    h_scr[...] = h
    y_ref[...] = (jax.nn.gelu(gr_ref[...]) * hs_scr[...]).astype(BF16)

    @pl.when(t == pl.num_programs(1) - 1)
    def _():
        hlast_ref[...] = h


def _rnn(z, conv0, h0, cw, cb, wr_b, br, wi_b, bi, lam, bsz, seq, tt):
    m = bsz * seq
    nt = seq // tt
    n_grp = D_RNN // MXU_DIM
    row = lambda b, t: (0, 0)
    return pl.pallas_call(
        _rnn_body,
        grid=(bsz, nt),
        in_specs=[pl.BlockSpec((None, tt, D_RNN), lambda b, t: (0, b * nt + t, 0)),
                  pl.BlockSpec((None, tt, D_RNN), lambda b, t: (1, b * nt + t, 0)),
                  pl.BlockSpec((None, CONV_W - 1, D_RNN), lambda b, t: (b, 0, 0)),
                  pl.BlockSpec((None, 1, D_RNN), lambda b, t: (b, 0, 0)),
                  pl.BlockSpec((CONV_W, D_RNN), row),
                  pl.BlockSpec((1, D_RNN), row),
                  pl.BlockSpec((n_grp, MXU_DIM, MXU_DIM), lambda b, t: (0, 0, 0)),
                  pl.BlockSpec((1, D_RNN), row),
                  pl.BlockSpec((n_grp, MXU_DIM, MXU_DIM), lambda b, t: (0, 0, 0)),
                  pl.BlockSpec((1, D_RNN), row),
                  pl.BlockSpec((1, D_RNN), row)],
        out_specs=[pl.BlockSpec((tt, D_RNN), lambda b, t: (b * nt + t, 0)),
                   pl.BlockSpec((None, 1, D_RNN), lambda b, t: (b, 0, 0))],
        out_shape=[_sds((m, D_RNN), BF16), _sds((bsz, 1, D_RNN), F32)],
        scratch_shapes=[pltpu.VMEM((tt + SUBLANES, D_RNN), F32),
                        pltpu.VMEM((tt, D_RNN), F32),
                        pltpu.VMEM((tt, D_RNN), F32),
                        pltpu.VMEM((1, D_RNN), F32),
                        pltpu.VMEM((tt, D_RNN), F32)],
        compiler_params=_cparams(("arbitrary", "arbitrary")),
        name="rnn",
    )(z, z, conv0, h0, cw, cb, wr_b, br, wi_b, bi, lam)


def _block_diag_groups(w):
    per = MXU_DIM // RNN_BLOCK
    n_grp = RNN_HEADS // per
    wg = w.reshape(n_grp, per, RNN_BLOCK, RNN_BLOCK)
    eye = jnp.eye(per, dtype=w.dtype)
    out = jnp.einsum('gpij,pq->gpiqj', wg, eye)
    return out.reshape(n_grp, MXU_DIM, MXU_DIM)


def _bucket(n):
    n_f = jnp.maximum(n, 1).astype(F32)
    large = MAX_EXACT + (jnp.log(n_f / MAX_EXACT) / math.log(MAX_DIST / MAX_EXACT)
                         * (N_BUCKETS - MAX_EXACT)).astype(jnp.int32)
    return jnp.where(n < MAX_EXACT, n, jnp.minimum(large, N_BUCKETS - 1))


def _lambda_value(lamp_ref, lam_init):
    lp = lamp_ref[...]
    s1 = jnp.sum(lp[0:1] * lp[1:2], axis=-1, keepdims=True)
    s2 = jnp.sum(lp[2:3] * lp[3:4], axis=-1, keepdims=True)
    return jnp.exp(s1) - jnp.exp(s2) + lam_init


def _table_lookup(tab, bucket):
    ids = jnp.arange(N_BUCKETS, dtype=bucket.dtype)[None, :, None, None]
    return jnp.sum(jnp.where(bucket[None, None] == ids, tab.T[:, :, None, None], 0.0), axis=1)


def _prompt_bias(rel_bias, tq):
    j = jnp.arange(tq)[:, None]
    i = jnp.arange(2 * tq)[None, :] % tq
    tab = rel_bias.astype(F32)
    tab = (tab - tab[N_BUCKETS - 1]) * LOG2E
    prev = _table_lookup(tab, _bucket(tq + i - j))[:, None]
    diag = _table_lookup(tab, _bucket(jnp.maximum(i - j, 0)))[:, None]
    diag = jnp.where(j <= i, diag, NEG_INF)
    slab = jnp.arange(4)[None, :, None, None]
    return jnp.where(slab == 0, 0.0, jnp.where(slab == 1, prev, jnp.where(slab == 2, diag, NEG_INF)))


def _pattn_body(q_ref, k_ref, v_ref, bias_ref, lamp_ref, g_ref, o_ref, vt_scr, s_scr, acc_scr,
                *, lam_init, n_hd):
    qi = pl.program_id(2)
    tq = q_ref.shape[0]
    tk = vt_scr.shape[-1]
    n_blk = k_ref.shape[0] // tk

    @pl.when(qi == 0)
    def _():
        def transpose_values(c, carry):
            start = pl.multiple_of(c * tk, tk)
            for hh in range(n_hd):
                vb = v_ref[pl.ds(start, tk), hh * ATT_VDIM:(hh + 1) * ATT_VDIM].astype(F32)
                vt_scr[hh, c, :ATT_VDIM, :] = vb.T.astype(BF16)
                vt_scr[hh, c, ATT_VDIM:, :] = jnp.ones((VT_PAD, tk), BF16)
            return carry

        lax.fori_loop(0, n_blk, transpose_values, 0)

    row = lax.broadcasted_iota(jnp.int32, (ATT_VDIM, tq), 0)
    qqt = []
    for hh in range(n_hd):
        qt = q_ref[:, hh * ATT_VDIM:(hh + 1) * ATT_VDIM].astype(F32).T
        qqt.append(jnp.concatenate([jnp.where(row < ATT_DH, qt, 0.0), jnp.where(row >= ATT_DH, qt, 0.0)],
                                   axis=1).astype(BF16))

    def scores(hh, ki):
        start = pl.multiple_of(jnp.minimum(ki, qi) * tk, tk)
        kb = k_ref[pl.ds(start, tk), hh * ATT_VDIM:(hh + 1) * ATT_VDIM]
        return jnp.dot(kb, qqt[hh], preferred_element_type=F32)

    def consume(hh, slot, ki, m):
        sel = jnp.clip(ki - (qi - 2), 0, 3)
        st = s_scr[hh, slot] + bias_ref[hh, sel]
        m_new = jnp.maximum(m, jnp.max(st, axis=0, keepdims=True))
        alpha = jnp.exp2(m - m_new)
        p = jnp.exp2(st - m_new)
        pv = jnp.dot(vt_scr[hh, jnp.minimum(ki, qi)], p.astype(BF16), preferred_element_type=F32)
        acc_scr[hh] = alpha * acc_scr[hh] + pv
        return m_new

    def pair_step(pi, stats):
        k0 = 2 * pi
        stats = list(stats)
        for hh in range(n_hd):
            s_scr[hh, 1] = scores(hh, k0 + 1)
        for hh in range(n_hd):
            stats[hh] = consume(hh, 0, k0, stats[hh])
        for hh in range(n_hd):
            s_scr[hh, 0] = scores(hh, k0 + 2)
        for hh in range(n_hd):
            stats[hh] = consume(hh, 1, k0 + 1, stats[hh])
        return tuple(stats)

    for hh in range(n_hd):
        s_scr[hh, 0] = scores(hh, 0)
        acc_scr[hh] = jnp.zeros(acc_scr.shape[1:], F32)
    init = tuple(jnp.full((1, 2 * tq), NEG_INF, F32) for _ in range(n_hd))
    lax.fori_loop(0, lax.shift_right_logical(qi + 2, 1), pair_step, init)
    lam = _lambda_value(lamp_ref, lam_init)
    for hh in range(n_hd):
        acc = acc_scr[hh]
        a = acc[:ATT_VDIM] / acc[ATT_VDIM:ATT_VDIM + 1]
        o = a[:, :tq] - lam * a[:, tq:]
        o = o * lax.rsqrt(jnp.mean(o * o, axis=0, keepdims=True) + LN_EPS) * g_ref[...] * (1.0 - lam_init)
        o_ref[:, hh * ATT_VDIM:(hh + 1) * ATT_VDIM] = o.T.astype(BF16)


def _prompt_attend(zb, bias, lamp, g_col, bsz, seq, tq, lam_init):
    m = bsz * seq
    nq = seq // tq
    n_hd = PROMPT_HEADS_PER_STEP
    wd = n_hd * ATT_VDIM
    return pl.pallas_call(
        functools.partial(_pattn_body, lam_init=lam_init, n_hd=n_hd),
        grid=(bsz, ATT_HEADS // n_hd, nq),
        in_specs=[pl.BlockSpec((None, tq, wd), lambda b, h, i: (0, b * nq + i, h)),
                  pl.BlockSpec((None, seq, wd), lambda b, h, i: (1, b, h)),
                  pl.BlockSpec((None, seq, wd), lambda b, h, i: (2, b, h)),
                  pl.BlockSpec((n_hd, 4, tq, 2 * tq), lambda b, h, i: (h, 0, 0, 0)),
                  pl.BlockSpec((4, ATT_DH), lambda b, h, i: (0, 0)),
                  pl.BlockSpec((ATT_VDIM, 1), lambda b, h, i: (0, 0))],
        out_specs=pl.BlockSpec((tq, wd), lambda b, h, i: (b * nq + i, h)),
        out_shape=_sds((m, ATT_HEADS * ATT_VDIM), BF16),
        scratch_shapes=[pltpu.VMEM((n_hd, seq // tq, ATT_VDIM + VT_PAD, tq), BF16),
                        pltpu.VMEM((n_hd, 2, tq, 2 * tq), F32),
                        pltpu.VMEM((n_hd, ATT_VDIM + VT_PAD, 2 * tq), F32)],
        compiler_params=_cparams(("arbitrary", "arbitrary", "arbitrary")),
        name="prompt_attn",
    )(zb, zb, zb, bias, lamp, g_col)


def _sattn_body(pt_ref, *refs, n_pp, lam_init):
    del pt_ref
    kp_refs = refs[:n_pp]
    vp_refs = refs[n_pp:2 * n_pp]
    (wq_ref, knew_ref, vnew_ref, blast_ref, bnew_ref, mask_ref, lamp_ref, g_ref,
     o_ref, m_scr, l_scr, acc_scr, s_scr) = refs[2 * n_pp:]
    c = pl.program_id(1)
    last = c == pl.num_programs(1) - 1
    rows = kp_refs[0].shape[0]
    n_tok = rows // ATT_HEADS

    @pl.when(c == 0)
    def _():
        m_scr[...] = jnp.full(m_scr.shape, NEG_INF, F32)
        l_scr[...] = jnp.zeros(l_scr.shape, F32)
        acc_scr[...] = jnp.zeros(acc_scr.shape, F32)

    valid = mask_ref[...] > 0.0

    def update(state, s3, vs):
        m_old, l, acc = state
        n_grp = len(vs)
        m2 = jnp.max(s3, axis=0)
        mx = m2[:, :LANES]
        for g in range(1, n_grp):
            mx = jnp.maximum(mx, m2[:, g * LANES:(g + 1) * LANES])
        m_new = jnp.maximum(m_old, mx)
        alpha = jnp.exp2(m_old - m_new)
        m_sub = jnp.where(valid, m_new, -NEG_INF)
        p = jnp.exp2(s3 - jnp.concatenate([m_sub] * n_grp, axis=1)[None])
        l2 = jnp.sum(p, axis=0)
        p2 = p.reshape(vs[0].shape[0], n_grp * LANES).astype(BF16)
        l = alpha * l
        acc = acc * jnp.sum(jnp.where(valid, alpha, 0.0), axis=0, keepdims=True)
        for g, vv in enumerate(vs):
            l = l + l2[:, g * LANES:(g + 1) * LANES]
            acc = acc + lax.dot_general(vv, p2[:, g * LANES:(g + 1) * LANES], (((0,), (0,)), ((), ())),
                                        preferred_element_type=F32)
        return m_new, l, acc

    wq2 = wq_ref[...]

    def pair_scores(pp):
        k2 = jnp.concatenate([kp_refs[2 * pp][...].astype(BF16), kp_refs[2 * pp + 1][...].astype(BF16)], axis=1)
        return jnp.dot(k2, wq2, preferred_element_type=F32)

    n_pairs = n_pp // 2
    state = (m_scr[...], l_scr[...], acc_scr[...])
    s_scr[0] = pair_scores(0)
    for pp in range(n_pairs):
        if pp + 1 < n_pairs:
            s_scr[(pp + 1) % 2] = pair_scores(pp + 1)
        s3 = s_scr[pp % 2].reshape(n_tok, ATT_HEADS, 2 * LANES)
        if pp == n_pairs - 1:
            s3 = s3 + jnp.where(last, blast_ref[...].reshape(n_tok, ATT_HEADS, 2 * LANES), 0.0)
        state = update(state, s3, [vp_refs[2 * pp][...].astype(BF16), vp_refs[2 * pp + 1][...].astype(BF16)])
    m_scr[...], l_scr[...], acc_scr[...] = state

    @pl.when(last)
    def _():
        n_new = knew_ref.shape[0] // ATT_HEADS
        sn = jnp.dot(knew_ref[...].astype(BF16), wq_ref[:LANES, :LANES], preferred_element_type=F32)
        sn = sn + bnew_ref[...]
        _, l, acc = update(state, sn.reshape(n_new, ATT_HEADS, LANES), [vnew_ref[...].astype(BF16)])
        l_row = jnp.sum(jnp.where(valid, l, 0.0), axis=0, keepdims=True)
        a = acc / l_row
        lam = _lambda_value(lamp_ref, lam_init)
        half = LANES // 2
        o = a[:, :half] - lam * a[:, half:]
        o = o * lax.rsqrt(jnp.mean(o * o, axis=0, keepdims=True) + LN_EPS) * g_ref[...] * (1.0 - lam_init)
        o_ref[...] = o


def _sample_attend(layer, page_table, cache_k4, cache_v4, wq2, knew, vnew, blast, bnew, mask,
                   lamp, g_col, n_pp, lam_init):
    bsz, n_pages = page_table.shape
    rows = cache_k4.shape[2]
    n_new = knew.shape[1]
    nc = n_pages // n_pp

    def page_spec(p):
        return pl.BlockSpec((None, None, rows, LANES),
                            lambda b, c, pt: (layer, pt[b, c * n_pp + p], 0, 0))

    const2 = lambda b, c, pt: (0, 0)
    in_specs = ([page_spec(p) for p in range(n_pp)] + [page_spec(p) for p in range(n_pp)] + [
        pl.BlockSpec((None, 2 * LANES, 2 * LANES), lambda b, c, pt: (b, 0, 0)),
        pl.BlockSpec((None, n_new, LANES), lambda b, c, pt: (b, 0, 0)),
        pl.BlockSpec((None, n_new, LANES), lambda b, c, pt: (b, 0, 0)),
        pl.BlockSpec((rows, 2 * LANES), const2),
        pl.BlockSpec((n_new, LANES), const2),
        pl.BlockSpec((SUBLANES, LANES), const2),
        pl.BlockSpec((4, ATT_DH), const2),
        pl.BlockSpec((ATT_VDIM, 1), const2)])
    grid_spec = pltpu.PrefetchScalarGridSpec(
        num_scalar_prefetch=1,
        grid=(bsz, nc),
        in_specs=in_specs,
        out_specs=pl.BlockSpec((None, ATT_VDIM, LANES // 2), lambda b, c, pt: (b, 0, 0)),
        scratch_shapes=[pltpu.VMEM((SUBLANES, LANES), F32),
                        pltpu.VMEM((SUBLANES, LANES), F32),
                        pltpu.VMEM((ATT_VDIM, LANES), F32),
                        pltpu.VMEM((2, rows, 2 * LANES), F32)])
    return pl.pallas_call(
        functools.partial(_sattn_body, n_pp=n_pp, lam_init=lam_init),
        grid_spec=grid_spec,
        out_shape=_sds((bsz, ATT_VDIM, LANES // 2), F32),
        compiler_params=_cparams(("arbitrary", "arbitrary")),
        name="sample_attn",
    )(page_table, *([cache_k4] * n_pp), *([cache_v4] * n_pp), wq2, knew, vnew, blast, bnew, mask,
      lamp, g_col)


def _sample_tables(rel_bias, page_size, dec_seq):
    tab = rel_bias.astype(F32)
    tab = (tab - tab[N_BUCKETS - 1]) * LOG2E
    col = jnp.arange(LANES)
    col_h = (col % (LANES // 2)) // dec_seq
    col_q = col % dec_seq
    row_h = jnp.arange(ATT_HEADS)
    mask = (row_h[:, None] == col_h[None, :]).astype(F32)
    t = jnp.arange(page_size)
    dist = page_size + col_q[None, :] - t[:, None]
    blast = _table_lookup(tab, _bucket(dist))
    blast = jnp.moveaxis(blast, 0, 1).reshape(page_size * ATT_HEADS, LANES)
    blast = jnp.concatenate([jnp.zeros_like(blast), blast], axis=1)
    j = jnp.arange(dec_seq)
    dn = col_q[None, :] - j[:, None]
    bnew = jnp.where((dn >= 0)[:, None, :],
                     jnp.moveaxis(_table_lookup(tab, _bucket(jnp.maximum(dn, 0))), 0, 1), NEG_INF)
    bnew = bnew.reshape(dec_seq * ATT_HEADS, LANES)
    return mask, blast, bnew


def _sample_wq(q_plane, bsz, dec_seq):
    q = q_plane.reshape(bsz, dec_seq, ATT_HEADS, 2, ATT_DH) * (QK_SCALE * LOG2E)
    eye = jnp.eye(2, dtype=q.dtype)
    w = jnp.einsum('bqhcd,ce->bcdehq', q, eye).reshape(bsz, LANES, LANES)
    w2 = jnp.einsum('bfc,pq->bpfqc', w, eye)
    return w2.reshape(bsz, 2 * LANES, 2 * LANES).astype(BF16)


def _top2_combine(x, rw_ref):
    xh = x.astype(BF16)
    xl = (x - xh.astype(F32)).astype(BF16)
    logits = (jnp.dot(xh, rw_ref[0], preferred_element_type=F32)
              + jnp.dot(xh, rw_ref[1], preferred_element_type=F32)
              + jnp.dot(xl, rw_ref[0], preferred_element_type=F32))
    lane = lax.broadcasted_iota(jnp.int32, logits.shape, 1).astype(F32)
    big = float(LANES)
    lg = jnp.where(lane < N_EXPERTS, logits, -jnp.inf)
    m1 = jnp.max(lg, axis=1, keepdims=True)
    i1 = jnp.min(jnp.where(lg == m1, lane, big), axis=1, keepdims=True)
    lg2 = jnp.where(lane == i1, -jnp.inf, lg)
    m2 = jnp.max(lg2, axis=1, keepdims=True)
    i2 = jnp.min(jnp.where(lg2 == m2, lane, big), axis=1, keepdims=True)
    e2 = jnp.exp(m2 - m1)
    den = 1.0 + e2
    return jnp.where(lane == i1, 1.0 / den, 0.0) + jnp.where(lane == i2, e2 / den, 0.0)


def _merge_body(yr_ref, ya_ref, ga_ref, gb_ref, x_ref, wr_ref, wa_ref, wo_ref, g_ref, b_ref, *rest,
                alpha, route):
    pr = jnp.dot(yr_ref[...], wr_ref[...], preferred_element_type=F32)
    pa = jnp.dot(ya_ref[...], wa_ref[...], preferred_element_type=F32)
    merged = jax.nn.sigmoid(ga_ref[...]) * pr + jax.nn.sigmoid(gb_ref[...]) * pa
    mix = jnp.dot(merged.astype(BF16), wo_ref[...], preferred_element_type=F32)
    x1 = _layer_norm_rows(alpha * x_ref[...] + mix, g_ref[...], b_ref[...])
    if route:
        rw_ref, o_ref, xb_ref, comb_ref = rest
        xb_ref[...] = x1.astype(BF16)
        comb_ref[...] = _top2_combine(x1, rw_ref)
    else:
        (o_ref,) = rest
    o_ref[...] = x1


def _merge(yr, ya, z, x, wr_b, wa_b, wo_b, g, b, tm, alpha, rw=None):
    m = x.shape[0]
    route = rw is not None
    rows = pl.BlockSpec((tm, D_MODEL), lambda i: (i, 0))
    full = pl.BlockSpec((D_MODEL, D_MODEL), lambda i: (0, 0))
    vec = pl.BlockSpec((1, D_MODEL), lambda i: (0, 0))
    in_specs = [rows, rows,
                pl.BlockSpec((None, tm, D_MODEL), lambda i: (3, i, 0)),
                pl.BlockSpec((None, tm, D_MODEL), lambda i: (4, i, 0)),
                rows, full, full, full, vec, vec]
    out_specs, out_shape, args = [rows], [_sds((m, D_MODEL), F32)], [yr, ya, z, z, x, wr_b, wa_b, wo_b, g, b]
    if route:
        in_specs.append(pl.BlockSpec((2, D_MODEL, LANES), lambda i: (0, 0, 0)))
        args.append(rw)
        out_specs += [rows, pl.BlockSpec((tm, LANES), lambda i: (i, 0))]
        out_shape += [_sds((m, D_MODEL), BF16), _sds((m, LANES), F32)]
    return pl.pallas_call(
        functools.partial(_merge_body, alpha=alpha, route=route),
        grid=(m // tm,),
        in_specs=in_specs,
        out_specs=out_specs,
        out_shape=out_shape,
        compiler_params=_cparams(("arbitrary",)),
        name="merge_route" if route else "merge",
    )(*args)


def _ple_ln(x, ffn, p_ref, pg_ref, pp_ref, g_ref, b_ref, xb, alpha):
    gate = jax.nn.sigmoid(jnp.dot(xb, pg_ref[...], preferred_element_type=F32))
    ple = gate * jnp.dot(p_ref[...].astype(BF16), pp_ref[...], preferred_element_type=F32)
    return _layer_norm_rows(alpha * x + ffn + ple, g_ref[...], b_ref[...])


def _ffn_body(x_ref, p_ref, w1_ref, w3_ref, w2_ref, pg_ref, pp_ref, g_ref, b_ref, o_ref,
              xb_scr, acc_scr, *, alpha):
    f = pl.program_id(1)

    @pl.when(f == 0)
    def _():
        xb_scr[...] = x_ref[...].astype(BF16)
        acc_scr[...] = jnp.zeros(acc_scr.shape, F32)

    xb = xb_scr[...]
    h1 = jnp.dot(xb, w1_ref[...], preferred_element_type=F32)
    h3 = jnp.dot(xb, w3_ref[...], preferred_element_type=F32)
    hh = (jax.nn.silu(h1) * h3).astype(BF16)
    acc_scr[...] += jnp.dot(hh, w2_ref[...], preferred_element_type=F32)

    @pl.when(f == pl.num_programs(1) - 1)
    def _():
        o_ref[...] = _ple_ln(x_ref[...], acc_scr[...], p_ref, pg_ref, pp_ref, g_ref, b_ref, xb, alpha)


def _ffn(x, p, w1_b, w3_b, w2_b, pg_b, pp_b, g, b, tm, tf, alpha):
    m = x.shape[0]
    d_ff = w1_b.shape[1]
    vec = pl.BlockSpec((1, D_MODEL), lambda i, f: (0, 0))
    return pl.pallas_call(
        functools.partial(_ffn_body, alpha=alpha),
        grid=(m // tm, d_ff // tf),
        in_specs=[pl.BlockSpec((tm, D_MODEL), lambda i, f: (i, 0)),
                  pl.BlockSpec((tm, PLE_DIM), lambda i, f: (i, 0)),
                  pl.BlockSpec((D_MODEL, tf), lambda i, f: (0, f)),
                  pl.BlockSpec((D_MODEL, tf), lambda i, f: (0, f)),
                  pl.BlockSpec((tf, D_MODEL), lambda i, f: (f, 0)),
                  pl.BlockSpec((D_MODEL, D_MODEL), lambda i, f: (0, 0)),
                  pl.BlockSpec((PLE_DIM, D_MODEL), lambda i, f: (0, 0)),
                  vec, vec],
        out_specs=pl.BlockSpec((tm, D_MODEL), lambda i, f: (i, 0)),
        out_shape=_sds((m, D_MODEL), F32),
        scratch_shapes=[pltpu.VMEM((tm, D_MODEL), BF16),
                        pltpu.VMEM((tm, D_MODEL), F32)],
        compiler_params=_cparams(("arbitrary", "arbitrary")),
        name="ffn_dense",
    )(x, p, w1_b, w3_b, w2_b, pg_b, pp_b, g, b)


def _moe_capacity(tm):
    cap = tm / 4 + 2.3 * math.sqrt(tm * 3 / 16)
    return min(tm, -(-int(math.ceil(cap)) // 16) * 16)


def _moe_body(xb_ref, comb_ref, w1_ref, w3_ref, w2_ref, o_ref, pos_scr, post_scr, combt_scr, xs_scr, ys_scr,
              *, cap):
    e = pl.program_id(1)
    f = pl.program_id(2)
    tm = xb_ref.shape[0]

    @pl.when(jnp.logical_and(e == 0, f == 0))
    def _():
        comb = comb_ref[...]
        sel = jnp.where(comb > 0.0, 1.0, 0.0).astype(BF16)
        r = lax.broadcasted_iota(jnp.int32, (tm, tm), 0)
        c = lax.broadcasted_iota(jnp.int32, (tm, tm), 1)
        pos_scr[...] = jnp.dot(jnp.where(c < r, 1.0, 0.0).astype(BF16), sel, preferred_element_type=F32)
        post_scr[...] = lax.dot_general(sel, jnp.where(r < c, 1.0, 0.0).astype(BF16), (((0,), (0,)), ((), ())),
                                        preferred_element_type=F32)
        combt_scr[...] = comb.T
        o_ref[...] = jnp.zeros(o_ref.shape, F32)

    lane = lax.broadcasted_iota(jnp.int32, (tm, LANES), 1)
    gate = jnp.sum(jnp.where(lane == e, comb_ref[...], 0.0), axis=1, keepdims=True)
    pos = jnp.sum(jnp.where(lane == e, pos_scr[...], 0.0), axis=1, keepdims=True)
    gate_t = combt_scr[pl.ds(e, 1), :]
    pos_t = post_scr[pl.ds(e, 1), :]
    n_rows = jnp.sum(jnp.where(gate_t > 0.0, 1.0, 0.0)).astype(jnp.int32)
    n_chunks = lax.div(n_rows + (cap - 1), cap)

    def chunk(ci, carry):
        base = (ci * cap).astype(F32)
        slot_col = lax.broadcasted_iota(jnp.int32, (1, cap), 1).astype(F32) + base
        slot_row = lax.broadcasted_iota(jnp.int32, (cap, 1), 0).astype(F32) + base
        scatter = jnp.where(jnp.logical_and(gate > 0.0, pos == slot_col), 1.0, 0.0).astype(BF16)

        @pl.when(f == 0)
        def _():
            gather = jnp.where(jnp.logical_and(gate_t > 0.0, pos_t == slot_row), 1.0, 0.0).astype(BF16)
            xs_scr[ci] = jnp.dot(gather, xb_ref[...], preferred_element_type=F32).astype(BF16)
            ys_scr[ci] = jnp.zeros(ys_scr.shape[1:], F32)

        xs = xs_scr[ci]
        h1 = jnp.dot(xs, w1_ref[...], preferred_element_type=F32)
        h3 = jnp.dot(xs, w3_ref[...], preferred_element_type=F32)
        hh = (jax.nn.silu(h1) * h3).astype(BF16)
        ys_scr[ci] += jnp.dot(hh, w2_ref[...], preferred_element_type=F32)

        @pl.when(f == pl.num_programs(2) - 1)
        def _():
            o_ref[...] += gate * jnp.dot(scatter, ys_scr[ci].astype(BF16), preferred_element_type=F32)

        return carry

    lax.fori_loop(0, n_chunks, chunk, 0)


def _moe(xb, comb, w1_b, w3_b, w2_b, tm, tf):
    m = xb.shape[0]
    n_e, _, d_ff = w1_b.shape
    cap = _moe_capacity(tm)
    n_slots = -(-tm // cap)
    return pl.pallas_call(
        functools.partial(_moe_body, cap=cap),
        grid=(m // tm, n_e, d_ff // tf),
        in_specs=[pl.BlockSpec((tm, D_MODEL), lambda i, e, f: (i, 0)),
                  pl.BlockSpec((tm, LANES), lambda i, e, f: (i, 0)),
                  pl.BlockSpec((None, D_MODEL, tf), lambda i, e, f: (e, 0, f)),
                  pl.BlockSpec((None, D_MODEL, tf), lambda i, e, f: (e, 0, f)),
                  pl.BlockSpec((None, tf, D_MODEL), lambda i, e, f: (e, f, 0))],
        out_specs=pl.BlockSpec((tm, D_MODEL), lambda i, e, f: (i, 0)),
        out_shape=_sds((m, D_MODEL), F32),
        scratch_shapes=[pltpu.VMEM((tm, LANES), F32),
                        pltpu.VMEM((LANES, tm), F32),
                        pltpu.VMEM((LANES, tm), F32),
                        pltpu.VMEM((n_slots, cap, D_MODEL), BF16),
                        pltpu.VMEM((n_slots, cap, D_MODEL), F32)],
        compiler_params=_cparams(("arbitrary", "arbitrary", "arbitrary")),
        name="moe",
    )(xb, comb, w1_b, w3_b, w2_b)


def _post_body(x_ref, f_ref, xb_ref, p_ref, pg_ref, pp_ref, g_ref, b_ref, o_ref, *, alpha):
    o_ref[...] = _ple_ln(x_ref[...], f_ref[...], p_ref, pg_ref, pp_ref, g_ref, b_ref, xb_ref[...], alpha)


def _post(x, ffn, xb, p, pg_b, pp_b, g, b, tm, alpha):
    m = x.shape[0]
    rows = pl.BlockSpec((tm, D_MODEL), lambda i: (i, 0))
    vec = pl.BlockSpec((1, D_MODEL), lambda i: (0, 0))
    return pl.pallas_call(
        functools.partial(_post_body, alpha=alpha),
        grid=(m // tm,),
        in_specs=[rows, rows, rows,
                  pl.BlockSpec((tm, PLE_DIM), lambda i: (i, 0)),
                  pl.BlockSpec((D_MODEL, D_MODEL), lambda i: (0, 0)),
                  pl.BlockSpec((PLE_DIM, D_MODEL), lambda i: (0, 0)),
                  vec, vec],
        out_specs=rows,
        out_shape=_sds((m, D_MODEL), F32),
        compiler_params=_cparams(("arbitrary",)),
        name="post",
    )(x, ffn, xb, p, pg_b, pp_b, g, b)


def _row_tile(m, pref):
    return pref if m % pref == 0 else m


def _trunk(x, p, conv_state, h_state, bsz, seq, w, attend):
    depth = w['w_in'].shape[0]
    m = bsz * seq
    alpha = (2.0 * depth) ** 0.25
    tm = _row_tile(m, 512)
    tt = _row_tile(seq, 256)
    xf = x.reshape(m, D_MODEL)
    hs, convs = [], []
    kv_stack = None
    for l in range(depth):
        lam_init = 0.8 - 0.6 * math.exp(-0.3 * l)
        z, zb, *kv_stack = _inproj(xf, w['w_in'][l], _row_tile(m, 2 * tm), l, depth, kv_stack)
        y_rnn, h_last = _rnn(z, conv_state[l], h_state[l].reshape(bsz, 1, D_RNN), w['conv_w'][l],
                             w['conv_b'][l].reshape(1, D_RNN), w['lru_wr'][l],
                             w['lru_br'][l].reshape(1, D_RNN), w['lru_wi'][l],
                             w['lru_bi'][l].reshape(1, D_RNN), w['lru_lambda'][l].reshape(1, D_RNN),
                             bsz, seq, tt)
        y_att = attend(l, z, zb, kv_stack, w['lamp'][l], w['subln_g'][l], lam_init)
        mi = l // 2
        moe = l % 2 == 1
        merged = _merge(y_rnn, y_att, z, xf, w['w_rnn_out'][l], w['w_attn_out'][l], w['w_o'][l],
                        w['ln1_g'][l].reshape(1, D_MODEL), w['ln1_b'][l].reshape(1, D_MODEL), tm, alpha,
                        rw=w['router_pad'][mi] if moe else None)
        pl_ = p[l].reshape(m, PLE_DIM)
        ln2 = (w['ln2_g'][l].reshape(1, D_MODEL), w['ln2_b'][l].reshape(1, D_MODEL))
        if moe:
            x1, x1b, comb = merged
            tmoe = _row_tile(m, 2 * tm)
            ffn = _moe(x1b, comb, w['moe_w1'][mi], w['moe_w3'][mi], w['moe_w2'][mi], tmoe,
                       w['moe_w1'].shape[3] // 2)
            xf = _post(x1, ffn, x1b, pl_, w['ple_gate_w'][l], w['ple_proj_w'][l], *ln2, tm, alpha)
        else:
            (x1,) = merged
            xf = _ffn(x1, pl_, w['ffn_w1'][mi], w['ffn_w3'][mi], w['ffn_w2'][mi], w['ple_gate_w'][l],
                      w['ple_proj_w'][l], *ln2, tm, w['ffn_w1'].shape[2] // 2, alpha)
        hs.append(h_last.reshape(bsz, D_RNN))
        xr = z[0].reshape(bsz, seq, D_RNN)
        xpad = jnp.concatenate([conv_state[l].astype(xr.dtype), xr], axis=1)
        convs.append(xpad[:, -(CONV_W - 1):])
    k_all, v_all = (t.reshape(depth, bsz, seq, ATT_HEADS, ATT_VDIM) for t in kv_stack)
    return xf.reshape(bsz, seq, D_MODEL), k_all, v_all, jnp.stack(hs), jnp.stack(convs)


def kernel(x_prompt, x_sample, cache_k, cache_v, state_h, state_conv, page_table, p_prompt, p_sample, rel_bias, w_in, conv_w, conv_b, lru_wr, lru_br, lru_wi, lru_bi, lru_lambda, w_rnn_out, w_attn_out, lam_q1, lam_k1, lam_q2, lam_k2, subln_g, w_o, ln1_g, ln1_b, ffn_w1, ffn_w3, ffn_w2, router_w, moe_w1, moe_w3, moe_w2, ple_gate_w, ple_proj_w, ln2_g, ln2_b):
    depth = w_in.shape[0]
    bsz, seq, _ = x_prompt.shape
    dbsz, dseq, _ = x_sample.shape
    n_pool, page_size = cache_k.shape[1], cache_k.shape[2]
    router_f32 = jnp.pad(router_w.astype(F32), ((0, 0), (0, 0), (0, LANES - N_EXPERTS)))
    router_hi = router_f32.astype(BF16)
    router_pad = jnp.stack([router_hi, (router_f32 - router_hi.astype(F32)).astype(BF16)], axis=1)
    w = dict(
        w_in=w_in.astype(BF16), conv_w=conv_w, conv_b=conv_b,
        lru_wr=jax.vmap(_block_diag_groups)(lru_wr).astype(BF16), lru_br=lru_br,
        lru_wi=jax.vmap(_block_diag_groups)(lru_wi).astype(BF16), lru_bi=lru_bi,
        lru_lambda=lru_lambda, w_rnn_out=w_rnn_out.astype(BF16), w_attn_out=w_attn_out.astype(BF16),
        lamp=jnp.stack([lam_q1, lam_k1, lam_q2, lam_k2], axis=1).astype(F32), subln_g=subln_g.astype(F32),
        w_o=w_o.astype(BF16), ln1_g=ln1_g, ln1_b=ln1_b,
        ffn_w1=ffn_w1.astype(BF16), ffn_w3=ffn_w3.astype(BF16), ffn_w2=ffn_w2.astype(BF16),
        router_pad=router_pad, moe_w1=moe_w1.astype(BF16), moe_w3=moe_w3.astype(BF16),
        moe_w2=moe_w2.astype(BF16), ple_gate_w=ple_gate_w.astype(BF16),
        ple_proj_w=ple_proj_w.astype(BF16), ln2_g=ln2_g, ln2_b=ln2_b)

    tq = _row_tile(seq, 256)
    assert tq >= MAX_DIST and page_size >= MAX_DIST
    pbias = _prompt_bias(rel_bias, tq)

    def prompt_attend(l, z, zb, kv_stack, lamp, g, lam_init):
        del l, z, kv_stack
        return _prompt_attend(zb, pbias, lamp, g.reshape(ATT_VDIM, 1), bsz, seq, tq, lam_init)

    conv0 = jnp.zeros((depth, bsz, CONV_W - 1, D_RNN), x_prompt.dtype)
    h0 = jnp.zeros((depth, bsz, D_RNN), F32)
    y_prompt, k_prompt, v_prompt, h_prompt, conv_prompt = _trunk(
        x_prompt, p_prompt, conv0, h0, bsz, seq, w, prompt_attend)

    rows = page_size * ATT_HEADS
    cache_k4 = cache_k.reshape(depth, n_pool, rows, ATT_VDIM)
    cache_v4 = cache_v.reshape(depth, n_pool, rows, ATT_VDIM)
    mask, blast, bnew = _sample_tables(rel_bias, page_size, dseq)
    n_pages = page_table.shape[1]
    n_pp = next(n for n in (16, 8, 4, 2) if n_pages % n == 0)

    def sample_attend(l, z, zb, kv_stack, lamp, g, lam_init):
        del zb
        wq = _sample_wq(z[2], dbsz, dseq)
        knew = kv_stack[0][l].reshape(dbsz, dseq * ATT_HEADS, ATT_VDIM)
        vnew = kv_stack[1][l].reshape(dbsz, dseq * ATT_HEADS, ATT_VDIM)
        o = _sample_attend(l, page_table, cache_k4, cache_v4, wq, knew, vnew, blast, bnew, mask,
                           lamp, g.reshape(ATT_VDIM, 1), n_pp, lam_init)
        o = o.reshape(dbsz, ATT_VDIM, ATT_HEADS, dseq)
        return jnp.transpose(o, (0, 3, 2, 1)).reshape(dbsz * dseq, ATT_HEADS * ATT_VDIM).astype(BF16)

    y_sample, k_sample, v_sample, h_sample, conv_sample = _trunk(
        x_sample, p_sample, state_conv, state_h, dbsz, dseq, w, sample_attend)
    return (y_prompt, y_sample, k_prompt, v_prompt, h_prompt, conv_prompt,
            k_sample, v_sample, h_sample, conv_sample)
```

```python
import functools
import math

import jax
import jax.numpy as jnp
from jax import lax
from jax.experimental import pallas as pl
from jax.experimental.pallas import tpu as pltpu

F32 = jnp.float32
BF16 = jnp.bfloat16

D_MODEL = 1024
D_RNN = 1024
RNN_HEADS = 16
RNN_BLOCK = D_RNN // RNN_HEADS
CONV_W = 4
LRU_C = 8.0
ATT_HEADS = 8
ATT_DH = 64
ATT_VDIM = 2 * ATT_DH
N_PLANES = 7
N_BUCKETS = 32
MAX_EXACT = N_BUCKETS // 2
MAX_DIST = 128
N_EXPERTS = 8
PLE_DIM = 256
LN_EPS = 1e-5
NEG_INF = -1e30
QK_SCALE = ATT_DH ** -0.5
LOG2E = math.log2(math.e)
VT_PAD = 16
PROMPT_HEADS_PER_STEP = 2

LANES = 128
SUBLANES = 8
MXU_DIM = 256
VMEM_LIMIT = 56 << 20


def _cparams(sem):
    return pltpu.CompilerParams(dimension_semantics=sem, vmem_limit_bytes=VMEM_LIMIT)


def _sds(shape, dtype):
    return jax.ShapeDtypeStruct(shape, dtype)


def _layer_norm_rows(y, g, b):
    mu = jnp.mean(y, axis=-1, keepdims=True)
    yc = y - mu
    var = jnp.mean(yc * yc, axis=-1, keepdims=True)
    return yc * lax.rsqrt(var + LN_EPS) * g + b


def _inproj_body(x_ref, w_ref, *rest):
    z_ref, zb_ref, k_ref, v_ref, xb_scr = rest[-5:]
    j = pl.program_id(1)

    @pl.when(j == 0)
    def _():
        xb_scr[...] = x_ref[...].astype(BF16)

    acc = jnp.dot(xb_scr[...], w_ref[...], preferred_element_type=F32)

    @pl.when(jnp.logical_or(j < 3, j > 4))
    def _():
        z_ref[...] = acc

    @pl.when(j == 2)
    def _():
        zb_ref[...] = (acc * (QK_SCALE * LOG2E)).astype(BF16)

    @pl.when(j == 3)
    def _():
        k_ref[...] = acc
        zb_ref[...] = acc.astype(BF16)

    @pl.when(j == 4)
    def _():
        v_ref[...] = acc
        zb_ref[...] = acc.astype(BF16)


def _inproj(x, w_b, tm, layer, depth, kv_stack):
    m = x.shape[0]
    stacked = pl.BlockSpec((None, tm, D_MODEL), lambda i, j: (layer, i, 0))
    in_specs = [pl.BlockSpec((tm, D_MODEL), lambda i, j: (i, 0)),
                pl.BlockSpec((D_MODEL, D_MODEL), lambda i, j: (0, j))]
    if kv_stack is None:
        kv_stack = [jnp.zeros((depth, m, D_MODEL), F32)] * 2
    in_specs += [pl.BlockSpec(memory_space=pl.ANY)] * 2
    return pl.pallas_call(
        _inproj_body,
        grid=(m // tm, N_PLANES),
        in_specs=in_specs,
        out_specs=[pl.BlockSpec((None, tm, D_MODEL), lambda i, j: (jnp.where(j < 3, j, jnp.maximum(j - 2, 2)), i, 0)),
                   pl.BlockSpec((None, tm, D_MODEL), lambda i, j: (jnp.clip(j - 2, 0, 2), i, 0)),
                   stacked, stacked],
        out_shape=[_sds((N_PLANES - 2, m, D_MODEL), F32), _sds((3, m, D_MODEL), BF16),
                   _sds((depth, m, D_MODEL), F32), _sds((depth, m, D_MODEL), F32)],
        scratch_shapes=[pltpu.VMEM((tm, D_MODEL), BF16)],
        input_output_aliases={2: 2, 3: 3},
        compiler_params=_cparams(("arbitrary", "arbitrary")),
        name="inproj",
    )(x, w_b, *kv_stack)


def _rnn_body(xr_ref, gr_ref, conv0_ref, h0_ref, cw_ref, cb_ref, wr_ref, br_ref, wi_ref, bi_ref,
              lam_ref, y_ref, hlast_ref, xpad_scr, a_scr, b_scr, h_scr, hs_scr):
    t = pl.program_id(1)
    tt = xr_ref.shape[0]
    halo = SUBLANES

    @pl.when(t == 0)
    def _():
        xpad_scr[halo - 3:halo, :] = conv0_ref[...]
        h_scr[...] = h0_ref[...]

    xr = xr_ref[...]
    xpad_scr[halo:halo + tt, :] = xr
    cw = cw_ref[...]
    xc = cb_ref[...] + xpad_scr[halo - 3:halo - 3 + tt, :] * cw[0:1]
    xc = xc + xpad_scr[halo - 2:halo - 2 + tt, :] * cw[1:2]
    xc = xc + xpad_scr[halo - 1:halo - 1 + tt, :] * cw[2:3]
    xc = xc + xr * cw[3:4]
    tail = xpad_scr[halo + tt - 3:halo + tt, :]
    xpad_scr[halo - 3:halo, :] = tail

    xcb = xc.astype(BF16)
    n_grp = D_RNN // MXU_DIM
    r_parts, i_parts = [], []
    for g in range(n_grp):
        xg = xcb[:, g * MXU_DIM:(g + 1) * MXU_DIM]
        r_parts.append(jnp.dot(xg, wr_ref[g], preferred_element_type=F32))
        i_parts.append(jnp.dot(xg, wi_ref[g], preferred_element_type=F32))
    r = jax.nn.sigmoid(jnp.concatenate(r_parts, axis=1) + br_ref[...])
    ig = jax.nn.sigmoid(jnp.concatenate(i_parts, axis=1) + bi_ref[...])
    log_a = -LRU_C * r * jax.nn.softplus(-lam_ref[...])
    a = jnp.exp(log_a)
    a_scr[...] = a
    b_scr[...] = jnp.sqrt(-jnp.tanh(log_a) * (a * a + 1.0)) * (ig * xc)

    def step(i, h):
        h = a_scr[pl.ds(i, 1), :] * h + b_scr[pl.ds(i, 1), :]
        hs_scr[pl.ds(i, 1), :] = h
        return h

    h = lax.fori_loop(0, tt, step, h_scr[...], unroll=8)
    h_scr[...] = h
    y_ref[...] = (jax.nn.gelu(gr_ref[...]) * hs_scr[...]).astype(BF16)

    @pl.when(t == pl.num_programs(1) - 1)
    def _():
        hlast_ref[...] = h


def _rnn(z, conv0, h0, cw, cb, wr_b, br, wi_b, bi, lam, bsz, seq, tt):
    m = bsz * seq
    nt = seq // tt
    n_grp = D_RNN // MXU_DIM
    row = lambda b, t: (0, 0)
    return pl.pallas_call(
        _rnn_body,
        grid=(bsz, nt),
        in_specs=[pl.BlockSpec((None, tt, D_RNN), lambda b, t: (0, b * nt + t, 0)),
                  pl.BlockSpec((None, tt, D_RNN), lambda b, t: (1, b * nt + t, 0)),
                  pl.BlockSpec((None, CONV_W - 1, D_RNN), lambda b, t: (b, 0, 0)),
                  pl.BlockSpec((None, 1, D_RNN), lambda b, t: (b, 0, 0)),
                  pl.BlockSpec((CONV_W, D_RNN), row),
                  pl.BlockSpec((1, D_RNN), row),
                  pl.BlockSpec((n_grp, MXU_DIM, MXU_DIM), lambda b, t: (0, 0, 0)),
                  pl.BlockSpec((1, D_RNN), row),
                  pl.BlockSpec((n_grp, MXU_DIM, MXU_DIM), lambda b, t: (0, 0, 0)),
                  pl.BlockSpec((1, D_RNN), row),
                  pl.BlockSpec((1, D_RNN), row)],
        out_specs=[pl.BlockSpec((tt, D_RNN), lambda b, t: (b * nt + t, 0)),
                   pl.BlockSpec((None, 1, D_RNN), lambda b, t: (b, 0, 0))],
        out_shape=[_sds((m, D_RNN), BF16), _sds((bsz, 1, D_RNN), F32)],
        scratch_shapes=[pltpu.VMEM((tt + SUBLANES, D_RNN), F32),
                        pltpu.VMEM((tt, D_RNN), F32),
                        pltpu.VMEM((tt, D_RNN), F32),
                        pltpu.VMEM((1, D_RNN), F32),
                        pltpu.VMEM((tt, D_RNN), F32)],
        compiler_params=_cparams(("arbitrary", "arbitrary")),
        name="rnn",
    )(z, z, conv0, h0, cw, cb, wr_b, br, wi_b, bi, lam)


def _block_diag_groups(w):
    per = MXU_DIM // RNN_BLOCK
    n_grp = RNN_HEADS // per
    wg = w.reshape(n_grp, per, RNN_BLOCK, RNN_BLOCK)
    eye = jnp.eye(per, dtype=w.dtype)
    out = jnp.einsum('gpij,pq->gpiqj', wg, eye)
    return out.reshape(n_grp, MXU_DIM, MXU_DIM)


def _bucket(n):
    n_f = jnp.maximum(n, 1).astype(F32)
    large = MAX_EXACT + (jnp.log(n_f / MAX_EXACT) / math.log(MAX_DIST / MAX_EXACT)
                         * (N_BUCKETS - MAX_EXACT)).astype(jnp.int32)
    return jnp.where(n < MAX_EXACT, n, jnp.minimum(large, N_BUCKETS - 1))


def _lambda_value(lamp_ref, lam_init):
    lp = lamp_ref[...]
    s1 = jnp.sum(lp[0:1] * lp[1:2], axis=-1, keepdims=True)
    s2 = jnp.sum(lp[2:3] * lp[3:4], axis=-1, keepdims=True)
    return jnp.exp(s1) - jnp.exp(s2) + lam_init


def _table_lookup(tab, bucket):
    ids = jnp.arange(N_BUCKETS, dtype=bucket.dtype)[None, :, None, None]
    return jnp.sum(jnp.where(bucket[None, None] == ids, tab.T[:, :, None, None], 0.0), axis=1)


def _prompt_bias(rel_bias, tq):
    j = jnp.arange(tq)[:, None]
    i = jnp.arange(2 * tq)[None, :] % tq
    tab = rel_bias.astype(F32)
    tab = (tab - tab[N_BUCKETS - 1]) * LOG2E
    prev = _table_lookup(tab, _bucket(tq + i - j))[:, None]
    diag = _table_lookup(tab, _bucket(jnp.maximum(i - j, 0)))[:, None]
    diag = jnp.where(j <= i, diag, NEG_INF)
    slab = jnp.arange(4)[None, :, None, None]
    return jnp.where(slab == 0, 0.0, jnp.where(slab == 1, prev, jnp.where(slab == 2, diag, NEG_INF)))


def _pattn_body(q_ref, k_ref, v_ref, bias_ref, lamp_ref, g_ref, o_ref, vt_scr, s_scr, acc_scr,
                *, lam_init, n_hd):
    qi = pl.program_id(2)
    tq = q_ref.shape[0]
    tk = vt_scr.shape[-1]
    n_blk = k_ref.shape[0] // tk

    @pl.when(qi == 0)
    def _():
        def transpose_values(c, carry):
            start = pl.multiple_of(c * tk, tk)
            for hh in range(n_hd):
                vb = v_ref[pl.ds(start, tk), hh * ATT_VDIM:(hh + 1) * ATT_VDIM].astype(F32)
                vt_scr[hh, c, :ATT_VDIM, :] = vb.T.astype(BF16)
                vt_scr[hh, c, ATT_VDIM:, :] = jnp.ones((VT_PAD, tk), BF16)
            return carry

        lax.fori_loop(0, n_blk, transpose_values, 0)

    row = lax.broadcasted_iota(jnp.int32, (ATT_VDIM, tq), 0)
    qqt = []
    for hh in range(n_hd):
        qt = q_ref[:, hh * ATT_VDIM:(hh + 1) * ATT_VDIM].astype(F32).T
        qqt.append(jnp.concatenate([jnp.where(row < ATT_DH, qt, 0.0), jnp.where(row >= ATT_DH, qt, 0.0)],
                                   axis=1).astype(BF16))

    def scores(hh, ki):
        start = pl.multiple_of(jnp.minimum(ki, qi) * tk, tk)
        kb = k_ref[pl.ds(start, tk), hh * ATT_VDIM:(hh + 1) * ATT_VDIM]
        return jnp.dot(kb, qqt[hh], preferred_element_type=F32)

    def consume(hh, slot, ki, m, near):
        st = s_scr[hh, slot]
        if near:
            sel = jnp.clip(ki - (qi - 2), 0, 3)
            st = st + bias_ref[hh, sel]
        m_new = jnp.maximum(m, jnp.max(st, axis=0, keepdims=True))
        alpha = jnp.exp2(m - m_new)
        p = jnp.exp2(st - m_new)
        pv = jnp.dot(vt_scr[hh, jnp.minimum(ki, qi)], p.astype(BF16), preferred_element_type=F32)
        acc_scr[hh] = alpha * acc_scr[hh] + pv
        return m_new

    def pair_step(pi, stats, near):
        k0 = 2 * pi
        stats = list(stats)
        for hh in range(n_hd):
            s_scr[hh, 1] = scores(hh, k0 + 1)
        for hh in range(n_hd):
            stats[hh] = consume(hh, 0, k0, stats[hh], near)
        for hh in range(n_hd):
            s_scr[hh, 0] = scores(hh, k0 + 2)
        for hh in range(n_hd):
            stats[hh] = consume(hh, 1, k0 + 1, stats[hh], near)
        return tuple(stats)

    for hh in range(n_hd):
        s_scr[hh, 0] = scores(hh, 0)
        acc_scr[hh] = jnp.zeros(acc_scr.shape[1:], F32)
    init = tuple(jnp.full((1, 2 * tq), NEG_INF, F32) for _ in range(n_hd))
    n_far_pairs = lax.shift_right_logical(jnp.maximum(qi - 1, 0), 1)
    stats = lax.fori_loop(0, n_far_pairs, functools.partial(pair_step, near=False), init)
    lax.fori_loop(n_far_pairs, lax.shift_right_logical(qi + 2, 1), functools.partial(pair_step, near=True), stats)
    lam = _lambda_value(lamp_ref, lam_init)
    for hh in range(n_hd):
        acc = acc_scr[hh]
        a = acc[:ATT_VDIM] / acc[ATT_VDIM:ATT_VDIM + 1]
        o = a[:, :tq] - lam * a[:, tq:]
        o = o * lax.rsqrt(jnp.mean(o * o, axis=0, keepdims=True) + LN_EPS) * g_ref[...] * (1.0 - lam_init)
        o_ref[:, hh * ATT_VDIM:(hh + 1) * ATT_VDIM] = o.T.astype(BF16)


def _prompt_attend(zb, bias, lamp, g_col, bsz, seq, tq, lam_init):
    m = bsz * seq
    nq = seq // tq
    n_hd = PROMPT_HEADS_PER_STEP
    wd = n_hd * ATT_VDIM
    return pl.pallas_call(
        functools.partial(_pattn_body, lam_init=lam_init, n_hd=n_hd),
        grid=(bsz, ATT_HEADS // n_hd, nq),
        in_specs=[pl.BlockSpec((None, tq, wd), lambda b, h, i: (0, b * nq + i, h)),
                  pl.BlockSpec((None, seq, wd), lambda b, h, i: (1, b, h)),
                  pl.BlockSpec((None, seq, wd), lambda b, h, i: (2, b, h)),
                  pl.BlockSpec((n_hd, 4, tq, 2 * tq), lambda b, h, i: (h, 0, 0, 0)),
                  pl.BlockSpec((4, ATT_DH), lambda b, h, i: (0, 0)),
                  pl.BlockSpec((ATT_VDIM, 1), lambda b, h, i: (0, 0))],
        out_specs=pl.BlockSpec((tq, wd), lambda b, h, i: (b * nq + i, h)),
        out_shape=_sds((m, ATT_HEADS * ATT_VDIM), BF16),
        scratch_shapes=[pltpu.VMEM((n_hd, seq // tq, ATT_VDIM + VT_PAD, tq), BF16),
                        pltpu.VMEM((n_hd, 2, tq, 2 * tq), F32),
                        pltpu.VMEM((n_hd, ATT_VDIM + VT_PAD, 2 * tq), F32)],
        compiler_params=_cparams(("arbitrary", "arbitrary", "arbitrary")),
        name="prompt_attn",
    )(zb, zb, zb, bias, lamp, g_col)


def _sattn_body(pt_ref, *refs, n_pp, lam_init):
    del pt_ref
    kp_refs = refs[:n_pp]
    vp_refs = refs[n_pp:2 * n_pp]
    (wq_ref, knew_ref, vnew_ref, blast_ref, bnew_ref, mask_ref, lamp_ref, g_ref,
     o_ref, m_scr, l_scr, acc_scr, s_scr) = refs[2 * n_pp:]
    c = pl.program_id(1)
    last = c == pl.num_programs(1) - 1
    rows = kp_refs[0].shape[0]
    n_tok = rows // ATT_HEADS

    @pl.when(c == 0)
    def _():
        m_scr[...] = jnp.full(m_scr.shape, NEG_INF, F32)
        l_scr[...] = jnp.zeros(l_scr.shape, F32)
        acc_scr[...] = jnp.zeros(acc_scr.shape, F32)

    valid = mask_ref[...] > 0.0

    def update(state, s3, vs):
        m_old, l, acc = state
        n_grp = len(vs)
        m2 = jnp.max(s3, axis=0)
        mx = m2[:, :LANES]
        for g in range(1, n_grp):
            mx = jnp.maximum(mx, m2[:, g * LANES:(g + 1) * LANES])
        m_new = jnp.maximum(m_old, mx)
        alpha = jnp.exp2(m_old - m_new)
        m_sub = jnp.where(valid, m_new, -NEG_INF)
        p = jnp.exp2(s3 - jnp.concatenate([m_sub] * n_grp, axis=1)[None])
        l2 = jnp.sum(p, axis=0)
        p2 = p.reshape(vs[0].shape[0], n_grp * LANES).astype(BF16)
        l = alpha * l
        acc = acc * jnp.sum(jnp.where(valid, alpha, 0.0), axis=0, keepdims=True)
        for g, vv in enumerate(vs):
            l = l + l2[:, g * LANES:(g + 1) * LANES]
            acc = acc + lax.dot_general(vv, p2[:, g * LANES:(g + 1) * LANES], (((0,), (0,)), ((), ())),
                                        preferred_element_type=F32)
        return m_new, l, acc

    wq2 = wq_ref[...]

    def pair_scores(pp):
        k2 = jnp.concatenate([kp_refs[2 * pp][...].astype(BF16), kp_refs[2 * pp + 1][...].astype(BF16)], axis=1)
        return jnp.dot(k2, wq2, preferred_element_type=F32)

    n_pairs = n_pp // 2
    state = (m_scr[...], l_scr[...], acc_scr[...])
    s_scr[0] = pair_scores(0)
    for pp in range(n_pairs):
        if pp + 1 < n_pairs:
            s_scr[(pp + 1) % 2] = pair_scores(pp + 1)
        s3 = s_scr[pp % 2].reshape(n_tok, ATT_HEADS, 2 * LANES)
        if pp == n_pairs - 1:
            s3 = s3 + jnp.where(last, blast_ref[...].reshape(n_tok, ATT_HEADS, 2 * LANES), 0.0)
        state = update(state, s3, [vp_refs[2 * pp][...].astype(BF16), vp_refs[2 * pp + 1][...].astype(BF16)])
    m_scr[...], l_scr[...], acc_scr[...] = state

    @pl.when(last)
    def _():
        n_new = knew_ref.shape[0] // ATT_HEADS
        sn = jnp.dot(knew_ref[...].astype(BF16), wq_ref[:LANES, :LANES], preferred_element_type=F32)
        sn = sn + bnew_ref[...]
        _, l, acc = update(state, sn.reshape(n_new, ATT_HEADS, LANES), [vnew_ref[...].astype(BF16)])
        l_row = jnp.sum(jnp.where(valid, l, 0.0), axis=0, keepdims=True)
        a = acc / l_row
        lam = _lambda_value(lamp_ref, lam_init)
        half = LANES // 2
        o = a[:, :half] - lam * a[:, half:]
        o = o * lax.rsqrt(jnp.mean(o * o, axis=0, keepdims=True) + LN_EPS) * g_ref[...] * (1.0 - lam_init)
        o_ref[...] = o


def _sample_attend(layer, page_table, cache_k4, cache_v4, wq2, knew, vnew, blast, bnew, mask,
                   lamp, g_col, n_pp, lam_init):
    bsz, n_pages = page_table.shape
    rows = cache_k4.shape[2]
    n_new = knew.shape[1]
    nc = n_pages // n_pp

    def page_spec(p):
        return pl.BlockSpec((None, None, rows, LANES),
                            lambda b, c, pt: (layer, pt[b, c * n_pp + p], 0, 0))

    const2 = lambda b, c, pt: (0, 0)
    in_specs = ([page_spec(p) for p in range(n_pp)] + [page_spec(p) for p in range(n_pp)] + [
        pl.BlockSpec((None, 2 * LANES, 2 * LANES), lambda b, c, pt: (b, 0, 0)),
        pl.BlockSpec((None, n_new, LANES), lambda b, c, pt: (b, 0, 0)),
        pl.BlockSpec((None, n_new, LANES), lambda b, c, pt: (b, 0, 0)),
        pl.BlockSpec((rows, 2 * LANES), const2),
        pl.BlockSpec((n_new, LANES), const2),
        pl.BlockSpec((SUBLANES, LANES), const2),
        pl.BlockSpec((4, ATT_DH), const2),
        pl.BlockSpec((ATT_VDIM, 1), const2)])
    grid_spec = pltpu.PrefetchScalarGridSpec(
        num_scalar_prefetch=1,
        grid=(bsz, nc),
        in_specs=in_specs,
        out_specs=pl.BlockSpec((None, ATT_VDIM, LANES // 2), lambda b, c, pt: (b, 0, 0)),
        scratch_shapes=[pltpu.VMEM((SUBLANES, LANES), F32),
                        pltpu.VMEM((SUBLANES, LANES), F32),
                        pltpu.VMEM((ATT_VDIM, LANES), F32),
                        pltpu.VMEM((2, rows, 2 * LANES), F32)])
    return pl.pallas_call(
        functools.partial(_sattn_body, n_pp=n_pp, lam_init=lam_init),
        grid_spec=grid_spec,
        out_shape=_sds((bsz, ATT_VDIM, LANES // 2), F32),
        compiler_params=_cparams(("arbitrary", "arbitrary")),
        name="sample_attn",
    )(page_table, *([cache_k4] * n_pp), *([cache_v4] * n_pp), wq2, knew, vnew, blast, bnew, mask,
      lamp, g_col)


def _sample_tables(rel_bias, page_size, dec_seq):
    tab = rel_bias.astype(F32)
    tab = (tab - tab[N_BUCKETS - 1]) * LOG2E
    col = jnp.arange(LANES)
    col_h = (col % (LANES // 2)) // dec_seq
    col_q = col % dec_seq
    row_h = jnp.arange(ATT_HEADS)
    mask = (row_h[:, None] == col_h[None, :]).astype(F32)
    t = jnp.arange(page_size)
    dist = page_size + col_q[None, :] - t[:, None]
    blast = _table_lookup(tab, _bucket(dist))
    blast = jnp.moveaxis(blast, 0, 1).reshape(page_size * ATT_HEADS, LANES)
    blast = jnp.concatenate([jnp.zeros_like(blast), blast], axis=1)
    j = jnp.arange(dec_seq)
    dn = col_q[None, :] - j[:, None]
    bnew = jnp.where((dn >= 0)[:, None, :],
                     jnp.moveaxis(_table_lookup(tab, _bucket(jnp.maximum(dn, 0))), 0, 1), NEG_INF)
    bnew = bnew.reshape(dec_seq * ATT_HEADS, LANES)
    return mask, blast, bnew


def _sample_wq(q_plane, bsz, dec_seq):
    q = q_plane.reshape(bsz, dec_seq, ATT_HEADS, 2, ATT_DH) * (QK_SCALE * LOG2E)
    eye = jnp.eye(2, dtype=q.dtype)
    w = jnp.einsum('bqhcd,ce->bcdehq', q, eye).reshape(bsz, LANES, LANES)
    w2 = jnp.einsum('bfc,pq->bpfqc', w, eye)
    return w2.reshape(bsz, 2 * LANES, 2 * LANES).astype(BF16)


def _top2_combine(x, rw_ref):
    xh = x.astype(BF16)
    xl = (x - xh.astype(F32)).astype(BF16)
    logits = (jnp.dot(xh, rw_ref[0], preferred_element_type=F32)
              + jnp.dot(xh, rw_ref[1], preferred_element_type=F32)
              + jnp.dot(xl, rw_ref[0], preferred_element_type=F32))
    lane = lax.broadcasted_iota(jnp.int32, logits.shape, 1).astype(F32)
    big = float(LANES)
    lg = jnp.where(lane < N_EXPERTS, logits, -jnp.inf)
    m1 = jnp.max(lg, axis=1, keepdims=True)
    i1 = jnp.min(jnp.where(lg == m1, lane, big), axis=1, keepdims=True)
    lg2 = jnp.where(lane == i1, -jnp.inf, lg)
    m2 = jnp.max(lg2, axis=1, keepdims=True)
    i2 = jnp.min(jnp.where(lg2 == m2, lane, big), axis=1, keepdims=True)
    e2 = jnp.exp(m2 - m1)
    den = 1.0 + e2
    return jnp.where(lane == i1, 1.0 / den, 0.0) + jnp.where(lane == i2, e2 / den, 0.0)


def _merge_body(yr_ref, ya_ref, ga_ref, gb_ref, x_ref, wr_ref, wa_ref, wo_ref, g_ref, b_ref, *rest,
                alpha, route):
    pr = jnp.dot(yr_ref[...], wr_ref[...], preferred_element_type=F32)
    pa = jnp.dot(ya_ref[...], wa_ref[...], preferred_element_type=F32)
    merged = jax.nn.sigmoid(ga_ref[...]) * pr + jax.nn.sigmoid(gb_ref[...]) * pa
    mix = jnp.dot(merged.astype(BF16), wo_ref[...], preferred_element_type=F32)
    x1 = _layer_norm_rows(alpha * x_ref[...] + mix, g_ref[...], b_ref[...])
    if route:
        rw_ref, o_ref, xb_ref, comb_ref = rest
        xb_ref[...] = x1.astype(BF16)
        comb_ref[...] = _top2_combine(x1, rw_ref)
    else:
        (o_ref,) = rest
    o_ref[...] = x1


def _merge(yr, ya, z, x, wr_b, wa_b, wo_b, g, b, tm, alpha, rw=None):
    m = x.shape[0]
    route = rw is not None
    rows = pl.BlockSpec((tm, D_MODEL), lambda i: (i, 0))
    full = pl.BlockSpec((D_MODEL, D_MODEL), lambda i: (0, 0))
    vec = pl.BlockSpec((1, D_MODEL), lambda i: (0, 0))
    in_specs = [rows, rows,
                pl.BlockSpec((None, tm, D_MODEL), lambda i: (3, i, 0)),
                pl.BlockSpec((None, tm, D_MODEL), lambda i: (4, i, 0)),
                rows, full, full, full, vec, vec]
    out_specs, out_shape, args = [rows], [_sds((m, D_MODEL), F32)], [yr, ya, z, z, x, wr_b, wa_b, wo_b, g, b]
    if route:
        in_specs.append(pl.BlockSpec((2, D_MODEL, LANES), lambda i: (0, 0, 0)))
        args.append(rw)
        out_specs += [rows, pl.BlockSpec((tm, LANES), lambda i: (i, 0))]
        out_shape += [_sds((m, D_MODEL), BF16), _sds((m, LANES), F32)]
    return pl.pallas_call(
        functools.partial(_merge_body, alpha=alpha, route=route),
        grid=(m // tm,),
        in_specs=in_specs,
        out_specs=out_specs,
        out_shape=out_shape,
        compiler_params=_cparams(("arbitrary",)),
        name="merge_route" if route else "merge",
    )(*args)


def _ple_ln(x, ffn, p_ref, pg_ref, pp_ref, g_ref, b_ref, xb, alpha):
    gate = jax.nn.sigmoid(jnp.dot(xb, pg_ref[...], preferred_element_type=F32))
    ple = gate * jnp.dot(p_ref[...].astype(BF16), pp_ref[...], preferred_element_type=F32)
    return _layer_norm_rows(alpha * x + ffn + ple, g_ref[...], b_ref[...])


def _ffn_body(x_ref, p_ref, w1_ref, w3_ref, w2_ref, pg_ref, pp_ref, g_ref, b_ref, o_ref,
              xb_scr, acc_scr, *, alpha):
    f = pl.program_id(1)

    @pl.when(f == 0)
    def _():
        xb_scr[...] = x_ref[...].astype(BF16)
        acc_scr[...] = jnp.zeros(acc_scr.shape, F32)

    xb = xb_scr[...]
    h1 = jnp.dot(xb, w1_ref[...], preferred_element_type=F32)
    h3 = jnp.dot(xb, w3_ref[...], preferred_element_type=F32)
    hh = (jax.nn.silu(h1) * h3).astype(BF16)
    acc_scr[...] += jnp.dot(hh, w2_ref[...], preferred_element_type=F32)

    @pl.when(f == pl.num_programs(1) - 1)
    def _():
        o_ref[...] = _ple_ln(x_ref[...], acc_scr[...], p_ref, pg_ref, pp_ref, g_ref, b_ref, xb, alpha)


def _ffn(x, p, w1_b, w3_b, w2_b, pg_b, pp_b, g, b, tm, tf, alpha):
    m = x.shape[0]
    d_ff = w1_b.shape[1]
    vec = pl.BlockSpec((1, D_MODEL), lambda i, f: (0, 0))
    return pl.pallas_call(
        functools.partial(_ffn_body, alpha=alpha),
        grid=(m // tm, d_ff // tf),
        in_specs=[pl.BlockSpec((tm, D_MODEL), lambda i, f: (i, 0)),
                  pl.BlockSpec((tm, PLE_DIM), lambda i, f: (i, 0)),
                  pl.BlockSpec((D_MODEL, tf), lambda i, f: (0, f)),
                  pl.BlockSpec((D_MODEL, tf), lambda i, f: (0, f)),
                  pl.BlockSpec((tf, D_MODEL), lambda i, f: (f, 0)),
                  pl.BlockSpec((D_MODEL, D_MODEL), lambda i, f: (0, 0)),
                  pl.BlockSpec((PLE_DIM, D_MODEL), lambda i, f: (0, 0)),
                  vec, vec],
        out_specs=pl.BlockSpec((tm, D_MODEL), lambda i, f: (i, 0)),
        out_shape=_sds((m, D_MODEL), F32),
        scratch_shapes=[pltpu.VMEM((tm, D_MODEL), BF16),
                        pltpu.VMEM((tm, D_MODEL), F32)],
        compiler_params=_cparams(("arbitrary", "arbitrary")),
        name="ffn_dense",
    )(x, p, w1_b, w3_b, w2_b, pg_b, pp_b, g, b)


def _moe_capacity(tm):
    cap = tm / 4 + 2.3 * math.sqrt(tm * 3 / 16)
    return min(tm, -(-int(math.ceil(cap)) // 16) * 16)


def _moe_body(xb_ref, comb_ref, w1_ref, w3_ref, w2_ref, o_ref, pos_scr, post_scr, combt_scr, xs_scr, ys_scr,
              *, cap):
    e = pl.program_id(1)
    f = pl.program_id(2)
    tm = xb_ref.shape[0]

    @pl.when(jnp.logical_and(e == 0, f == 0))
    def _():
        comb = comb_ref[...]
        sel = jnp.where(comb > 0.0, 1.0, 0.0).astype(BF16)
        r = lax.broadcasted_iota(jnp.int32, (tm, tm), 0)
        c = lax.broadcasted_iota(jnp.int32, (tm, tm), 1)
        pos_scr[...] = jnp.dot(jnp.where(c < r, 1.0, 0.0).astype(BF16), sel, preferred_element_type=F32)
        post_scr[...] = lax.dot_general(sel, jnp.where(r < c, 1.0, 0.0).astype(BF16), (((0,), (0,)), ((), ())),
                                        preferred_element_type=F32)
        combt_scr[...] = comb.T
        o_ref[...] = jnp.zeros(o_ref.shape, F32)

    lane = lax.broadcasted_iota(jnp.int32, (tm, LANES), 1)
    gate = jnp.sum(jnp.where(lane == e, comb_ref[...], 0.0), axis=1, keepdims=True)
    pos = jnp.sum(jnp.where(lane == e, pos_scr[...], 0.0), axis=1, keepdims=True)
    gate_t = combt_scr[pl.ds(e, 1), :]
    pos_t = post_scr[pl.ds(e, 1), :]
    n_rows = jnp.sum(jnp.where(gate_t > 0.0, 1.0, 0.0)).astype(jnp.int32)
    n_chunks = lax.div(n_rows + (cap - 1), cap)

    def chunk(ci, carry):
        base = (ci * cap).astype(F32)
        slot_col = lax.broadcasted_iota(jnp.int32, (1, cap), 1).astype(F32) + base
        slot_row = lax.broadcasted_iota(jnp.int32, (cap, 1), 0).astype(F32) + base
        scatter = jnp.where(jnp.logical_and(gate > 0.0, pos == slot_col), 1.0, 0.0).astype(BF16)

        @pl.when(f == 0)
        def _():
            gather = jnp.where(jnp.logical_and(gate_t > 0.0, pos_t == slot_row), 1.0, 0.0).astype(BF16)
            xs_scr[ci] = jnp.dot(gather, xb_ref[...], preferred_element_type=F32).astype(BF16)
            ys_scr[ci] = jnp.zeros(ys_scr.shape[1:], F32)

        xs = xs_scr[ci]
        h1 = jnp.dot(xs, w1_ref[...], preferred_element_type=F32)
        h3 = jnp.dot(xs, w3_ref[...], preferred_element_type=F32)
        hh = (jax.nn.silu(h1) * h3).astype(BF16)
        ys_scr[ci] += jnp.dot(hh, w2_ref[...], preferred_element_type=F32)

        @pl.when(f == pl.num_programs(2) - 1)
        def _():
            o_ref[...] += gate * jnp.dot(scatter, ys_scr[ci].astype(BF16), preferred_element_type=F32)

        return carry

    lax.fori_loop(0, n_chunks, chunk, 0)


def _moe(xb, comb, w1_b, w3_b, w2_b, tm, tf):
    m = xb.shape[0]
    n_e, _, d_ff = w1_b.shape
    cap = _moe_capacity(tm)
    n_slots = -(-tm // cap)
    return pl.pallas_call(
        functools.partial(_moe_body, cap=cap),
        grid=(m // tm, n_e, d_ff // tf),
        in_specs=[pl.BlockSpec((tm, D_MODEL), lambda i, e, f: (i, 0)),
                  pl.BlockSpec((tm, LANES), lambda i, e, f: (i, 0)),
                  pl.BlockSpec((None, D_MODEL, tf), lambda i, e, f: (e, 0, f)),
                  pl.BlockSpec((None, D_MODEL, tf), lambda i, e, f: (e, 0, f)),
                  pl.BlockSpec((None, tf, D_MODEL), lambda i, e, f: (e, f, 0))],
        out_specs=pl.BlockSpec((tm, D_MODEL), lambda i, e, f: (i, 0)),
        out_shape=_sds((m, D_MODEL), F32),
        scratch_shapes=[pltpu.VMEM((tm, LANES), F32),
                        pltpu.VMEM((LANES, tm), F32),
                        pltpu.VMEM((LANES, tm), F32),
                        pltpu.VMEM((n_slots, cap, D_MODEL), BF16),
                        pltpu.VMEM((n_slots, cap, D_MODEL), F32)],
        compiler_params=_cparams(("arbitrary", "arbitrary", "arbitrary")),
        name="moe",
    )(xb, comb, w1_b, w3_b, w2_b)


def _post_body(x_ref, f_ref, xb_ref, p_ref, pg_ref, pp_ref, g_ref, b_ref, o_ref, *, alpha):
    o_ref[...] = _ple_ln(x_ref[...], f_ref[...], p_ref, pg_ref, pp_ref, g_ref, b_ref, xb_ref[...], alpha)


def _post(x, ffn, xb, p, pg_b, pp_b, g, b, tm, alpha):
    m = x.shape[0]
    rows = pl.BlockSpec((tm, D_MODEL), lambda i: (i, 0))
    vec = pl.BlockSpec((1, D_MODEL), lambda i: (0, 0))
    return pl.pallas_call(
        functools.partial(_post_body, alpha=alpha),
        grid=(m // tm,),
        in_specs=[rows, rows, rows,
                  pl.BlockSpec((tm, PLE_DIM), lambda i: (i, 0)),
                  pl.BlockSpec((D_MODEL, D_MODEL), lambda i: (0, 0)),
                  pl.BlockSpec((PLE_DIM, D_MODEL), lambda i: (0, 0)),
                  vec, vec],
        out_specs=rows,
        out_shape=_sds((m, D_MODEL), F32),
        compiler_params=_cparams(("arbitrary",)),
        name="post",
    )(x, ffn, xb, p, pg_b, pp_b, g, b)


def _row_tile(m, pref):
    return pref if m % pref == 0 else m


def _trunk(x, p, conv_state, h_state, bsz, seq, w, attend):
    depth = w['w_in'].shape[0]
    m = bsz * seq
    alpha = (2.0 * depth) ** 0.25
    tm = _row_tile(m, 512)
    tt = _row_tile(seq, 256)
    xf = x.reshape(m, D_MODEL)
    hs, convs = [], []
    kv_stack = None
    for l in range(depth):
        lam_init = 0.8 - 0.6 * math.exp(-0.3 * l)
        z, zb, *kv_stack = _inproj(xf, w['w_in'][l], _row_tile(m, 2 * tm), l, depth, kv_stack)
        y_rnn, h_last = _rnn(z, conv_state[l], h_state[l].reshape(bsz, 1, D_RNN), w['conv_w'][l],
                             w['conv_b'][l].reshape(1, D_RNN), w['lru_wr'][l],
                             w['lru_br'][l].reshape(1, D_RNN), w['lru_wi'][l],
                             w['lru_bi'][l].reshape(1, D_RNN), w['lru_lambda'][l].reshape(1, D_RNN),
                             bsz, seq, tt)
        y_att = attend(l, z, zb, kv_stack, w['lamp'][l], w['subln_g'][l], lam_init)
        mi = l // 2
        moe = l % 2 == 1
        merged = _merge(y_rnn, y_att, z, xf, w['w_rnn_out'][l], w['w_attn_out'][l], w['w_o'][l],
                        w['ln1_g'][l].reshape(1, D_MODEL), w['ln1_b'][l].reshape(1, D_MODEL), tm, alpha,
                        rw=w['router_pad'][mi] if moe else None)
        pl_ = p[l].reshape(m, PLE_DIM)
        ln2 = (w['ln2_g'][l].reshape(1, D_MODEL), w['ln2_b'][l].reshape(1, D_MODEL))
        if moe:
            x1, x1b, comb = merged
            tmoe = _row_tile(m, 2 * tm)
            ffn = _moe(x1b, comb, w['moe_w1'][mi], w['moe_w3'][mi], w['moe_w2'][mi], tmoe,
                       w['moe_w1'].shape[3] // 2)
            xf = _post(x1, ffn, x1b, pl_, w['ple_gate_w'][l], w['ple_proj_w'][l], *ln2, tm, alpha)
        else:
            (x1,) = merged
            xf = _ffn(x1, pl_, w['ffn_w1'][mi], w['ffn_w3'][mi], w['ffn_w2'][mi], w['ple_gate_w'][l],
                      w['ple_proj_w'][l], *ln2, tm, w['ffn_w1'].shape[2] // 2, alpha)
        hs.append(h_last.reshape(bsz, D_RNN))
        n_tail = min(seq, CONV_W - 1)
        xr_tail = z[0].reshape(bsz, seq, D_RNN)[:, seq - n_tail:]
        xpad = jnp.concatenate([conv_state[l].astype(xr_tail.dtype), xr_tail], axis=1)
        convs.append(xpad[:, -(CONV_W - 1):])
    k_all, v_all = (t.reshape(depth, bsz, seq, ATT_HEADS, ATT_VDIM) for t in kv_stack)
    return xf.reshape(bsz, seq, D_MODEL), k_all, v_all, jnp.stack(hs), jnp.stack(convs)


def kernel(x_prompt, x_sample, cache_k, cache_v, state_h, state_conv, page_table, p_prompt, p_sample, rel_bias, w_in, conv_w, conv_b, lru_wr, lru_br, lru_wi, lru_bi, lru_lambda, w_rnn_out, w_attn_out, lam_q1, lam_k1, lam_q2, lam_k2, subln_g, w_o, ln1_g, ln1_b, ffn_w1, ffn_w3, ffn_w2, router_w, moe_w1, moe_w3, moe_w2, ple_gate_w, ple_proj_w, ln2_g, ln2_b):
    depth = w_in.shape[0]
    bsz, seq, _ = x_prompt.shape
    dbsz, dseq, _ = x_sample.shape
    n_pool, page_size = cache_k.shape[1], cache_k.shape[2]
    router_f32 = jnp.pad(router_w.astype(F32), ((0, 0), (0, 0), (0, LANES - N_EXPERTS)))
    router_hi = router_f32.astype(BF16)
    router_pad = jnp.stack([router_hi, (router_f32 - router_hi.astype(F32)).astype(BF16)], axis=1)
    w = dict(
        w_in=w_in.astype(BF16), conv_w=conv_w, conv_b=conv_b,
        lru_wr=jax.vmap(_block_diag_groups)(lru_wr).astype(BF16), lru_br=lru_br,
        lru_wi=jax.vmap(_block_diag_groups)(lru_wi).astype(BF16), lru_bi=lru_bi,
        lru_lambda=lru_lambda, w_rnn_out=w_rnn_out.astype(BF16), w_attn_out=w_attn_out.astype(BF16),
        lamp=jnp.stack([lam_q1, lam_k1, lam_q2, lam_k2], axis=1).astype(F32), subln_g=subln_g.astype(F32),
        w_o=w_o.astype(BF16), ln1_g=ln1_g, ln1_b=ln1_b,
        ffn_w1=ffn_w1.astype(BF16), ffn_w3=ffn_w3.astype(BF16), ffn_w2=ffn_w2.astype(BF16),
        router_pad=router_pad, moe_w1=moe_w1.astype(BF16), moe_w3=moe_w3.astype(BF16),
        moe_w2=moe_w2.astype(BF16), ple_gate_w=ple_gate_w.astype(BF16),
        ple_proj_w=ple_proj_w.astype(BF16), ln2_g=ln2_g, ln2_b=ln2_b)

    tq = _row_tile(seq, 256)
    assert tq >= MAX_DIST and page_size >= MAX_DIST
    pbias = _prompt_bias(rel_bias, tq)

    def prompt_attend(l, z, zb, kv_stack, lamp, g, lam_init):
        del l, z, kv_stack
        return _prompt_attend(zb, pbias, lamp, g.reshape(ATT_VDIM, 1), bsz, seq, tq, lam_init)

    conv0 = jnp.zeros((depth, bsz, CONV_W - 1, D_RNN), x_prompt.dtype)
    h0 = jnp.zeros((depth, bsz, D_RNN), F32)
    y_prompt, k_prompt, v_prompt, h_prompt, conv_prompt = _trunk(
        x_prompt, p_prompt, conv0, h0, bsz, seq, w, prompt_attend)

    rows = page_size * ATT_HEADS
    cache_k4 = cache_k.reshape(depth, n_pool, rows, ATT_VDIM)
    cache_v4 = cache_v.reshape(depth, n_pool, rows, ATT_VDIM)
    mask, blast, bnew = _sample_tables(rel_bias, page_size, dseq)
    n_pages = page_table.shape[1]
    n_pp = next(n for n in (16, 8, 4, 2) if n_pages % n == 0)

    def sample_attend(l, z, zb, kv_stack, lamp, g, lam_init):
        del zb
        wq = _sample_wq(z[2], dbsz, dseq)
        knew = kv_stack[0][l].reshape(dbsz, dseq * ATT_HEADS, ATT_VDIM)
        vnew = kv_stack[1][l].reshape(dbsz, dseq * ATT_HEADS, ATT_VDIM)
        o = _sample_attend(l, page_table, cache_k4, cache_v4, wq, knew, vnew, blast, bnew, mask,
                           lamp, g.reshape(ATT_VDIM, 1), n_pp, lam_init)
        o = o.reshape(dbsz, ATT_VDIM, ATT_HEADS, dseq)
        return jnp.transpose(o, (0, 3, 2, 1)).reshape(dbsz * dseq, ATT_HEADS * ATT_VDIM).astype(BF16)

    y_sample, k_sample, v_sample, h_sample, conv_sample = _trunk(
        x_sample, p_sample, state_conv, state_h, dbsz, dseq, w, sample_attend)
    return (y_prompt, y_sample, k_prompt, v_prompt, h_prompt, conv_prompt,
            k_sample, v_sample, h_sample, conv_sample)
```

```python
import functools
import math

import jax
import jax.numpy as jnp
from jax import lax
from jax.experimental import pallas as pl
from jax.experimental.pallas import tpu as pltpu

F32 = jnp.float32
BF16 = jnp.bfloat16

D_MODEL = 1024
D_RNN = 1024
RNN_HEADS = 16
RNN_BLOCK = D_RNN // RNN_HEADS
CONV_W = 4
LRU_C = 8.0
ATT_HEADS = 8
ATT_DH = 64
ATT_VDIM = 2 * ATT_DH
N_PLANES = 7
N_BUCKETS = 32
MAX_EXACT = N_BUCKETS // 2
MAX_DIST = 128
N_EXPERTS = 8
PLE_DIM = 256
LN_EPS = 1e-5
NEG_INF = -1e30
QK_SCALE = ATT_DH ** -0.5
LOG2E = math.log2(math.e)
VT_PAD = 16
PROMPT_HEADS_PER_STEP = 2

LANES = 128
SUBLANES = 8
MXU_DIM = 256
VMEM_LIMIT = 56 << 20


def _cparams(sem):
    return pltpu.CompilerParams(dimension_semantics=sem, vmem_limit_bytes=VMEM_LIMIT)


def _sds(shape, dtype):
    return jax.ShapeDtypeStruct(shape, dtype)


def _layer_norm_rows(y, g, b):
    mu = jnp.mean(y, axis=-1, keepdims=True)
    yc = y - mu
    var = jnp.mean(yc * yc, axis=-1, keepdims=True)
    return yc * lax.rsqrt(var + LN_EPS) * g + b


def _inproj_body(x_ref, w_ref, *rest):
    z_ref, zb_ref, k_ref, v_ref, xb_scr = rest[-5:]
    j = pl.program_id(1)

    @pl.when(j == 0)
    def _():
        xb_scr[...] = x_ref[...].astype(BF16)

    acc = jnp.dot(xb_scr[...], w_ref[...], preferred_element_type=F32)

    @pl.when(jnp.logical_or(j < 3, j > 4))
    def _():
        z_ref[...] = acc

    @pl.when(j == 2)
    def _():
        zb_ref[...] = (acc * (QK_SCALE * LOG2E)).astype(BF16)

    @pl.when(j == 3)
    def _():
        k_ref[...] = acc
        zb_ref[...] = acc.astype(BF16)

    @pl.when(j == 4)
    def _():
        v_ref[...] = acc
        zb_ref[...] = acc.astype(BF16)


def _inproj(x, w_b, tm, layer, depth, kv_stack):
    m = x.shape[0]
    stacked = pl.BlockSpec((None, tm, D_MODEL), lambda i, j: (layer, i, 0))
    in_specs = [pl.BlockSpec((tm, D_MODEL), lambda i, j: (i, 0)),
                pl.BlockSpec((D_MODEL, D_MODEL), lambda i, j: (0, j))]
    if kv_stack is None:
        kv_stack = [jnp.zeros((depth, m, D_MODEL), F32)] * 2
    in_specs += [pl.BlockSpec(memory_space=pl.ANY)] * 2
    return pl.pallas_call(
        _inproj_body,
        grid=(m // tm, N_PLANES),
        in_specs=in_specs,
        out_specs=[pl.BlockSpec((None, tm, D_MODEL), lambda i, j: (jnp.where(j < 3, j, jnp.maximum(j - 2, 2)), i, 0)),
                   pl.BlockSpec((None, tm, D_MODEL), lambda i, j: (jnp.clip(j - 2, 0, 2), i, 0)),
                   stacked, stacked],
        out_shape=[_sds((N_PLANES - 2, m, D_MODEL), F32), _sds((3, m, D_MODEL), BF16),
                   _sds((depth, m, D_MODEL), F32), _sds((depth, m, D_MODEL), F32)],
        scratch_shapes=[pltpu.VMEM((tm, D_MODEL), BF16)],
        input_output_aliases={2: 2, 3: 3},
        compiler_params=_cparams(("arbitrary", "arbitrary")),
        name="inproj",
    )(x, w_b, *kv_stack)


def _rnn_body(xr_ref, gr_ref, conv0_ref, h0_ref, cw_ref, cb_ref, wr_ref, br_ref, wi_ref, bi_ref,
              lam_ref, y_ref, hlast_ref, convlast_ref, xpad_scr, a_scr, b_scr, h_scr, hs_scr):
    t = pl.program_id(1)
    tt = xr_ref.shape[0]
    halo = SUBLANES

    @pl.when(t == 0)
    def _():
        xpad_scr[halo - 3:halo, :] = conv0_ref[...]
        h_scr[...] = h0_ref[...]

    xr = xr_ref[...]
    xpad_scr[halo:halo + tt, :] = xr
    cw = cw_ref[...]
    xc = cb_ref[...] + xpad_scr[halo - 3:halo - 3 + tt, :] * cw[0:1]
    xc = xc + xpad_scr[halo - 2:halo - 2 + tt, :] * cw[1:2]
    xc = xc + xpad_scr[halo - 1:halo - 1 + tt, :] * cw[2:3]
    xc = xc + xr * cw[3:4]
    tail = xpad_scr[halo + tt - 3:halo + tt, :]
    xpad_scr[halo - 3:halo, :] = tail

    xcb = xc.astype(BF16)
    n_grp = D_RNN // MXU_DIM
    r_parts, i_parts = [], []
    for g in range(n_grp):
        xg = xcb[:, g * MXU_DIM:(g + 1) * MXU_DIM]
        r_parts.append(jnp.dot(xg, wr_ref[g], preferred_element_type=F32))
        i_parts.append(jnp.dot(xg, wi_ref[g], preferred_element_type=F32))
    r = jax.nn.sigmoid(jnp.concatenate(r_parts, axis=1) + br_ref[...])
    ig = jax.nn.sigmoid(jnp.concatenate(i_parts, axis=1) + bi_ref[...])
    log_a = -LRU_C * r * jax.nn.softplus(-lam_ref[...])
    a = jnp.exp(log_a)
    a_scr[...] = a
    b_scr[...] = jnp.sqrt(-jnp.tanh(log_a) * (a * a + 1.0)) * (ig * xc)

    def step(i, h):
        h = a_scr[pl.ds(i, 1), :] * h + b_scr[pl.ds(i, 1), :]
        hs_scr[pl.ds(i, 1), :] = h
        return h

    h = lax.fori_loop(0, tt, step, h_scr[...], unroll=8)
    h_scr[...] = h
    y_ref[...] = (jax.nn.gelu(gr_ref[...]) * hs_scr[...]).astype(BF16)

    @pl.when(t == pl.num_programs(1) - 1)
    def _():
        hlast_ref[...] = h
        convlast_ref[...] = tail


def _rnn(z, conv0, h0, cw, cb, wr_b, br, wi_b, bi, lam, bsz, seq, tt):
    m = bsz * seq
    nt = seq // tt
    n_grp = D_RNN // MXU_DIM
    row = lambda b, t: (0, 0)
    return pl.pallas_call(
        _rnn_body,
        grid=(bsz, nt),
        in_specs=[pl.BlockSpec((None, tt, D_RNN), lambda b, t: (0, b * nt + t, 0)),
                  pl.BlockSpec((None, tt, D_RNN), lambda b, t: (1, b * nt + t, 0)),
                  pl.BlockSpec((None, CONV_W - 1, D_RNN), lambda b, t: (b, 0, 0)),
                  pl.BlockSpec((None, 1, D_RNN), lambda b, t: (b, 0, 0)),
                  pl.BlockSpec((CONV_W, D_RNN), row),
                  pl.BlockSpec((1, D_RNN), row),
                  pl.BlockSpec((n_grp, MXU_DIM, MXU_DIM), lambda b, t: (0, 0, 0)),
                  pl.BlockSpec((1, D_RNN), row),
                  pl.BlockSpec((n_grp, MXU_DIM, MXU_DIM), lambda b, t: (0, 0, 0)),
                  pl.BlockSpec((1, D_RNN), row),
                  pl.BlockSpec((1, D_RNN), row)],
        out_specs=[pl.BlockSpec((tt, D_RNN), lambda b, t: (b * nt + t, 0)),
                   pl.BlockSpec((None, 1, D_RNN), lambda b, t: (b, 0, 0)),
                   pl.BlockSpec((None, CONV_W - 1, D_RNN), lambda b, t: (b, 0, 0))],
        out_shape=[_sds((m, D_RNN), BF16), _sds((bsz, 1, D_RNN), F32), _sds((bsz, CONV_W - 1, D_RNN), F32)],
        scratch_shapes=[pltpu.VMEM((tt + SUBLANES, D_RNN), F32),
                        pltpu.VMEM((tt, D_RNN), F32),
                        pltpu.VMEM((tt, D_RNN), F32),
                        pltpu.VMEM((1, D_RNN), F32),
                        pltpu.VMEM((tt, D_RNN), F32)],
        compiler_params=_cparams(("arbitrary", "arbitrary")),
        name="rnn",
    )(z, z, conv0, h0, cw, cb, wr_b, br, wi_b, bi, lam)


def _block_diag_groups(w):
    per = MXU_DIM // RNN_BLOCK
    n_grp = RNN_HEADS // per
    wg = w.reshape(n_grp, per, RNN_BLOCK, RNN_BLOCK)
    eye = jnp.eye(per, dtype=w.dtype)
    out = jnp.einsum('gpij,pq->gpiqj', wg, eye)
    return out.reshape(n_grp, MXU_DIM, MXU_DIM)


def _bucket(n):
    n_f = jnp.maximum(n, 1).astype(F32)
    large = MAX_EXACT + (jnp.log(n_f / MAX_EXACT) / math.log(MAX_DIST / MAX_EXACT)
                         * (N_BUCKETS - MAX_EXACT)).astype(jnp.int32)
    return jnp.where(n < MAX_EXACT, n, jnp.minimum(large, N_BUCKETS - 1))


def _lambda_value(lamp_ref, lam_init):
    lp = lamp_ref[...]
    s1 = jnp.sum(lp[0:1] * lp[1:2], axis=-1, keepdims=True)
    s2 = jnp.sum(lp[2:3] * lp[3:4], axis=-1, keepdims=True)
    return jnp.exp(s1) - jnp.exp(s2) + lam_init


def _table_lookup(tab, bucket):
    ids = jnp.arange(N_BUCKETS, dtype=bucket.dtype)[None, :, None, None]
    return jnp.sum(jnp.where(bucket[None, None] == ids, tab.T[:, :, None, None], 0.0), axis=1)


def _prompt_bias(rel_bias, tq):
    j = jnp.arange(tq)[:, None]
    i = jnp.arange(2 * tq)[None, :] % tq
    tab = rel_bias.astype(F32)
    tab = (tab - tab[N_BUCKETS - 1]) * LOG2E
    prev = _table_lookup(tab, _bucket(tq + i - j))[:, None]
    diag = _table_lookup(tab, _bucket(jnp.maximum(i - j, 0)))[:, None]
    diag = jnp.where(j <= i, diag, NEG_INF)
    slab = jnp.arange(4)[None, :, None, None]
    return jnp.where(slab == 0, 0.0, jnp.where(slab == 1, prev, jnp.where(slab == 2, diag, NEG_INF)))


def _pattn_body(q_ref, k_ref, v_ref, bias_ref, lamp_ref, g_ref, o_ref, vt_scr, s_scr, acc_scr,
                *, lam_init, n_hd):
    qi = pl.program_id(2)
    tq = q_ref.shape[0]
    tk = vt_scr.shape[-1]
    n_blk = k_ref.shape[0] // tk

    @pl.when(qi == 0)
    def _():
        def transpose_values(c, carry):
            start = pl.multiple_of(c * tk, tk)
            for hh in range(n_hd):
                vb = v_ref[pl.ds(start, tk), hh * ATT_VDIM:(hh + 1) * ATT_VDIM].astype(F32)
                vt_scr[hh, c, :ATT_VDIM, :] = vb.T.astype(BF16)
                vt_scr[hh, c, ATT_VDIM:, :] = jnp.ones((VT_PAD, tk), BF16)
            return carry

        lax.fori_loop(0, n_blk, transpose_values, 0)

    row = lax.broadcasted_iota(jnp.int32, (ATT_VDIM, tq), 0)
    qqt = []
    for hh in range(n_hd):
        qt = q_ref[:, hh * ATT_VDIM:(hh + 1) * ATT_VDIM].astype(F32).T
        qqt.append(jnp.concatenate([jnp.where(row < ATT_DH, qt, 0.0), jnp.where(row >= ATT_DH, qt, 0.0)],
                                   axis=1).astype(BF16))

    def scores(hh, ki):
        start = pl.multiple_of(jnp.minimum(ki, qi) * tk, tk)
        kb = k_ref[pl.ds(start, tk), hh * ATT_VDIM:(hh + 1) * ATT_VDIM]
        return jnp.dot(kb, qqt[hh], preferred_element_type=F32)

    def consume(hh, slot, ki, m, near):
        st = s_scr[hh, slot]
        if near:
            sel = jnp.clip(ki - (qi - 2), 0, 3)
            st = st + bias_ref[hh, sel]
        m_new = jnp.maximum(m, jnp.max(st, axis=0, keepdims=True))
        alpha = jnp.exp2(m - m_new)
        p = jnp.exp2(st - m_new)
        pv = jnp.dot(vt_scr[hh, jnp.minimum(ki, qi)], p.astype(BF16), preferred_element_type=F32)
        acc_scr[hh] = alpha * acc_scr[hh] + pv
        return m_new

    def pair_step(pi, stats, near):
        k0 = 2 * pi
        stats = list(stats)
        for hh in range(n_hd):
            s_scr[hh, 1] = scores(hh, k0 + 1)
        for hh in range(n_hd):
            stats[hh] = consume(hh, 0, k0, stats[hh], near)
        for hh in range(n_hd):
            s_scr[hh, 0] = scores(hh, k0 + 2)
        for hh in range(n_hd):
            stats[hh] = consume(hh, 1, k0 + 1, stats[hh], near)
        return tuple(stats)

    for hh in range(n_hd):
        s_scr[hh, 0] = scores(hh, 0)
        acc_scr[hh] = jnp.zeros(acc_scr.shape[1:], F32)
    init = tuple(jnp.full((1, 2 * tq), NEG_INF, F32) for _ in range(n_hd))
    n_far_pairs = lax.shift_right_logical(jnp.maximum(qi - 1, 0), 1)
    stats = lax.fori_loop(0, n_far_pairs, functools.partial(pair_step, near=False), init)
    lax.fori_loop(n_far_pairs, lax.shift_right_logical(qi + 2, 1), functools.partial(pair_step, near=True), stats)
    lam = _lambda_value(lamp_ref, lam_init)
    for hh in range(n_hd):
        acc = acc_scr[hh]
        a = acc[:ATT_VDIM] / acc[ATT_VDIM:ATT_VDIM + 1]
        o = a[:, :tq] - lam * a[:, tq:]
        o = o * lax.rsqrt(jnp.mean(o * o, axis=0, keepdims=True) + LN_EPS) * g_ref[...] * (1.0 - lam_init)
        o_ref[:, hh * ATT_VDIM:(hh + 1) * ATT_VDIM] = o.T.astype(BF16)


def _prompt_attend(zb, bias, lamp, g_col, bsz, seq, tq, lam_init):
    m = bsz * seq
    nq = seq // tq
    n_hd = PROMPT_HEADS_PER_STEP
    wd = n_hd * ATT_VDIM
    return pl.pallas_call(
        functools.partial(_pattn_body, lam_init=lam_init, n_hd=n_hd),
        grid=(bsz, ATT_HEADS // n_hd, nq),
        in_specs=[pl.BlockSpec((None, tq, wd), lambda b, h, i: (0, b * nq + i, h)),
                  pl.BlockSpec((None, seq, wd), lambda b, h, i: (1, b, h)),
                  pl.BlockSpec((None, seq, wd), lambda b, h, i: (2, b, h)),
                  pl.BlockSpec((n_hd, 4, tq, 2 * tq), lambda b, h, i: (h, 0, 0, 0)),
                  pl.BlockSpec((4, ATT_DH), lambda b, h, i: (0, 0)),
                  pl.BlockSpec((ATT_VDIM, 1), lambda b, h, i: (0, 0))],
        out_specs=pl.BlockSpec((tq, wd), lambda b, h, i: (b * nq + i, h)),
        out_shape=_sds((m, ATT_HEADS * ATT_VDIM), BF16),
        scratch_shapes=[pltpu.VMEM((n_hd, seq // tq, ATT_VDIM + VT_PAD, tq), BF16),
                        pltpu.VMEM((n_hd, 2, tq, 2 * tq), F32),
                        pltpu.VMEM((n_hd, ATT_VDIM + VT_PAD, 2 * tq), F32)],
        compiler_params=_cparams(("arbitrary", "arbitrary", "arbitrary")),
        name="prompt_attn",
    )(zb, zb, zb, bias, lamp, g_col)


def _sattn_body(pt_ref, *refs, n_pp, lam_init):
    del pt_ref
    kp_refs = refs[:n_pp]
    vp_refs = refs[n_pp:2 * n_pp]
    (wq_ref, knew_ref, vnew_ref, blast_ref, bnew_ref, mask_ref, lamp_ref, g_ref,
     o_ref, m_scr, l_scr, acc_scr, s_scr) = refs[2 * n_pp:]
    c = pl.program_id(1)
    last = c == pl.num_programs(1) - 1
    rows = kp_refs[0].shape[0]
    n_tok = rows // ATT_HEADS

    @pl.when(c == 0)
    def _():
        m_scr[...] = jnp.full(m_scr.shape, NEG_INF, F32)
        l_scr[...] = jnp.zeros(l_scr.shape, F32)
        acc_scr[...] = jnp.zeros(acc_scr.shape, F32)

    valid = mask_ref[...] > 0.0

    def update(state, s3, vs):
        m_old, l, acc = state
        n_grp = len(vs)
        m2 = jnp.max(s3, axis=0)
        mx = m2[:, :LANES]
        for g in range(1, n_grp):
            mx = jnp.maximum(mx, m2[:, g * LANES:(g + 1) * LANES])
        m_new = jnp.maximum(m_old, mx)
        alpha = jnp.exp2(m_old - m_new)
        m_sub = jnp.where(valid, m_new, -NEG_INF)
        p = jnp.exp2(s3 - jnp.concatenate([m_sub] * n_grp, axis=1)[None])
        l2 = jnp.sum(p, axis=0)
        p2 = p.reshape(vs[0].shape[0], n_grp * LANES).astype(BF16)
        l = alpha * l
        acc = acc * jnp.sum(jnp.where(valid, alpha, 0.0), axis=0, keepdims=True)
        for g, vv in enumerate(vs):
            l = l + l2[:, g * LANES:(g + 1) * LANES]
            acc = acc + lax.dot_general(vv, p2[:, g * LANES:(g + 1) * LANES], (((0,), (0,)), ((), ())),
                                        preferred_element_type=F32)
        return m_new, l, acc

    wq2 = wq_ref[...]

    def pair_scores(pp):
        k2 = jnp.concatenate([kp_refs[2 * pp][...].astype(BF16), kp_refs[2 * pp + 1][...].astype(BF16)], axis=1)
        return jnp.dot(k2, wq2, preferred_element_type=F32)

    n_pairs = n_pp // 2
    state = (m_scr[...], l_scr[...], acc_scr[...])
    s_scr[0] = pair_scores(0)
    for pp in range(n_pairs):
        if pp + 1 < n_pairs:
            s_scr[(pp + 1) % 2] = pair_scores(pp + 1)
        s3 = s_scr[pp % 2].reshape(n_tok, ATT_HEADS, 2 * LANES)
        if pp == n_pairs - 1:
            s3 = s3 + jnp.where(last, blast_ref[...].reshape(n_tok, ATT_HEADS, 2 * LANES), 0.0)
        state = update(state, s3, [vp_refs[2 * pp][...].astype(BF16), vp_refs[2 * pp + 1][...].astype(BF16)])
    m_scr[...], l_scr[...], acc_scr[...] = state

    @pl.when(last)
    def _():
        n_new = knew_ref.shape[0] // ATT_HEADS
        sn = jnp.dot(knew_ref[...].astype(BF16), wq_ref[:LANES, :LANES], preferred_element_type=F32)
        sn = sn + bnew_ref[...]
        _, l, acc = update(state, sn.reshape(n_new, ATT_HEADS, LANES), [vnew_ref[...].astype(BF16)])
        l_row = jnp.sum(jnp.where(valid, l, 0.0), axis=0, keepdims=True)
        a = acc / l_row
        lam = _lambda_value(lamp_ref, lam_init)
        half = LANES // 2
        o = a[:, :half] - lam * a[:, half:]
        o = o * lax.rsqrt(jnp.mean(o * o, axis=0, keepdims=True) + LN_EPS) * g_ref[...] * (1.0 - lam_init)
        o_ref[...] = o


def _sample_attend(layer, page_table, cache_k4, cache_v4, wq2, knew, vnew, blast, bnew, mask,
                   lamp, g_col, n_pp, lam_init):
    bsz, n_pages = page_table.shape
    rows = cache_k4.shape[2]
    n_new = knew.shape[1]
    nc = n_pages // n_pp

    def page_spec(p):
        return pl.BlockSpec((None, None, rows, LANES),
                            lambda b, c, pt: (layer, pt[b, c * n_pp + p], 0, 0))

    const2 = lambda b, c, pt: (0, 0)
    in_specs = ([page_spec(p) for p in range(n_pp)] + [page_spec(p) for p in range(n_pp)] + [
        pl.BlockSpec((None, 2 * LANES, 2 * LANES), lambda b, c, pt: (b, 0, 0)),
        pl.BlockSpec((None, n_new, LANES), lambda b, c, pt: (b, 0, 0)),
        pl.BlockSpec((None, n_new, LANES), lambda b, c, pt: (b, 0, 0)),
        pl.BlockSpec((rows, 2 * LANES), const2),
        pl.BlockSpec((n_new, LANES), const2),
        pl.BlockSpec((SUBLANES, LANES), const2),
        pl.BlockSpec((4, ATT_DH), const2),
        pl.BlockSpec((ATT_VDIM, 1), const2)])
    grid_spec = pltpu.PrefetchScalarGridSpec(
        num_scalar_prefetch=1,
        grid=(bsz, nc),
        in_specs=in_specs,
        out_specs=pl.BlockSpec((None, ATT_VDIM, LANES // 2), lambda b, c, pt: (b, 0, 0)),
        scratch_shapes=[pltpu.VMEM((SUBLANES, LANES), F32),
                        pltpu.VMEM((SUBLANES, LANES), F32),
                        pltpu.VMEM((ATT_VDIM, LANES), F32),
                        pltpu.VMEM((2, rows, 2 * LANES), F32)])
    return pl.pallas_call(
        functools.partial(_sattn_body, n_pp=n_pp, lam_init=lam_init),
        grid_spec=grid_spec,
        out_shape=_sds((bsz, ATT_VDIM, LANES // 2), F32),
        compiler_params=_cparams(("arbitrary", "arbitrary")),
        name="sample_attn",
    )(page_table, *([cache_k4] * n_pp), *([cache_v4] * n_pp), wq2, knew, vnew, blast, bnew, mask,
      lamp, g_col)


def _sample_tables(rel_bias, page_size, dec_seq):
    tab = rel_bias.astype(F32)
    tab = (tab - tab[N_BUCKETS - 1]) * LOG2E
    col = jnp.arange(LANES)
    col_h = (col % (LANES // 2)) // dec_seq
    col_q = col % dec_seq
    row_h = jnp.arange(ATT_HEADS)
    mask = (row_h[:, None] == col_h[None, :]).astype(F32)
    t = jnp.arange(page_size)
    dist = page_size + col_q[None, :] - t[:, None]
    blast = _table_lookup(tab, _bucket(dist))
    blast = jnp.moveaxis(blast, 0, 1).reshape(page_size * ATT_HEADS, LANES)
    blast = jnp.concatenate([jnp.zeros_like(blast), blast], axis=1)
    j = jnp.arange(dec_seq)
    dn = col_q[None, :] - j[:, None]
    bnew = jnp.where((dn >= 0)[:, None, :],
                     jnp.moveaxis(_table_lookup(tab, _bucket(jnp.maximum(dn, 0))), 0, 1), NEG_INF)
    bnew = bnew.reshape(dec_seq * ATT_HEADS, LANES)
    return mask, blast, bnew


def _sample_wq(q_plane, bsz, dec_seq):
    q = q_plane.reshape(bsz, dec_seq, ATT_HEADS, 2, ATT_DH) * (QK_SCALE * LOG2E)
    eye = jnp.eye(2, dtype=q.dtype)
    w = jnp.einsum('bqhcd,ce->bcdehq', q, eye).reshape(bsz, LANES, LANES)
    w2 = jnp.einsum('bfc,pq->bpfqc', w, eye)
    return w2.reshape(bsz, 2 * LANES, 2 * LANES).astype(BF16)


def _top2_combine(x, rw_ref):
    xh = x.astype(BF16)
    xl = (x - xh.astype(F32)).astype(BF16)
    logits = (jnp.dot(xh, rw_ref[0], preferred_element_type=F32)
              + jnp.dot(xh, rw_ref[1], preferred_element_type=F32)
              + jnp.dot(xl, rw_ref[0], preferred_element_type=F32))
    lane = lax.broadcasted_iota(jnp.int32, logits.shape, 1).astype(F32)
    big = float(LANES)
    lg = jnp.where(lane < N_EXPERTS, logits, -jnp.inf)
    m1 = jnp.max(lg, axis=1, keepdims=True)
    i1 = jnp.min(jnp.where(lg == m1, lane, big), axis=1, keepdims=True)
    lg2 = jnp.where(lane == i1, -jnp.inf, lg)
    m2 = jnp.max(lg2, axis=1, keepdims=True)
    i2 = jnp.min(jnp.where(lg2 == m2, lane, big), axis=1, keepdims=True)
    e2 = jnp.exp(m2 - m1)
    den = 1.0 + e2
    return jnp.where(lane == i1, 1.0 / den, 0.0) + jnp.where(lane == i2, e2 / den, 0.0)


def _merge_body(yr_ref, ya_ref, ga_ref, gb_ref, x_ref, wr_ref, wa_ref, wo_ref, g_ref, b_ref, *rest,
                alpha, route):
    pr = jnp.dot(yr_ref[...], wr_ref[...], preferred_element_type=F32)
    pa = jnp.dot(ya_ref[...], wa_ref[...], preferred_element_type=F32)
    merged = jax.nn.sigmoid(ga_ref[...]) * pr + jax.nn.sigmoid(gb_ref[...]) * pa
    mix = jnp.dot(merged.astype(BF16), wo_ref[...], preferred_element_type=F32)
    x1 = _layer_norm_rows(alpha * x_ref[...] + mix, g_ref[...], b_ref[...])
    if route:
        rw_ref, o_ref, xb_ref, comb_ref = rest
        xb_ref[...] = x1.astype(BF16)
        comb_ref[...] = _top2_combine(x1, rw_ref)
    else:
        (o_ref,) = rest
    o_ref[...] = x1


def _merge(yr, ya, z, x, wr_b, wa_b, wo_b, g, b, tm, alpha, rw=None):
    m = x.shape[0]
    route = rw is not None
    rows = pl.BlockSpec((tm, D_MODEL), lambda i: (i, 0))
    full = pl.BlockSpec((D_MODEL, D_MODEL), lambda i: (0, 0))
    vec = pl.BlockSpec((1, D_MODEL), lambda i: (0, 0))
    in_specs = [rows, rows,
                pl.BlockSpec((None, tm, D_MODEL), lambda i: (3, i, 0)),
                pl.BlockSpec((None, tm, D_MODEL), lambda i: (4, i, 0)),
                rows, full, full, full, vec, vec]
    out_specs, out_shape, args = [rows], [_sds((m, D_MODEL), F32)], [yr, ya, z, z, x, wr_b, wa_b, wo_b, g, b]
    if route:
        in_specs.append(pl.BlockSpec((2, D_MODEL, LANES), lambda i: (0, 0, 0)))
        args.append(rw)
        out_specs += [rows, pl.BlockSpec((tm, LANES), lambda i: (i, 0))]
        out_shape += [_sds((m, D_MODEL), BF16), _sds((m, LANES), F32)]
    return pl.pallas_call(
        functools.partial(_merge_body, alpha=alpha, route=route),
        grid=(m // tm,),
        in_specs=in_specs,
        out_specs=out_specs,
        out_shape=out_shape,
        compiler_params=_cparams(("arbitrary",)),
        name="merge_route" if route else "merge",
    )(*args)


def _ple_ln(x, ffn, p_ref, pg_ref, pp_ref, g_ref, b_ref, xb, alpha):
    gate = jax.nn.sigmoid(jnp.dot(xb, pg_ref[...], preferred_element_type=F32))
    ple = gate * jnp.dot(p_ref[...].astype(BF16), pp_ref[...], preferred_element_type=F32)
    return _layer_norm_rows(alpha * x + ffn + ple, g_ref[...], b_ref[...])


def _ffn_body(x_ref, p_ref, w1_ref, w3_ref, w2_ref, pg_ref, pp_ref, g_ref, b_ref, o_ref,
              xb_scr, acc_scr, *, alpha):
    f = pl.program_id(1)

    @pl.when(f == 0)
    def _():
        xb_scr[...] = x_ref[...].astype(BF16)
        acc_scr[...] = jnp.zeros(acc_scr.shape, F32)

    xb = xb_scr[...]
    h1 = jnp.dot(xb, w1_ref[...], preferred_element_type=F32)
    h3 = jnp.dot(xb, w3_ref[...], preferred_element_type=F32)
    hh = (jax.nn.silu(h1) * h3).astype(BF16)
    acc_scr[...] += jnp.dot(hh, w2_ref[...], preferred_element_type=F32)

    @pl.when(f == pl.num_programs(1) - 1)
    def _():
        o_ref[...] = _ple_ln(x_ref[...], acc_scr[...], p_ref, pg_ref, pp_ref, g_ref, b_ref, xb, alpha)


def _ffn(x, p, w1_b, w3_b, w2_b, pg_b, pp_b, g, b, tm, tf, alpha):
    m = x.shape[0]
    d_ff = w1_b.shape[1]
    vec = pl.BlockSpec((1, D_MODEL), lambda i, f: (0, 0))
    return pl.pallas_call(
        functools.partial(_ffn_body, alpha=alpha),
        grid=(m // tm, d_ff // tf),
        in_specs=[pl.BlockSpec((tm, D_MODEL), lambda i, f: (i, 0)),
                  pl.BlockSpec((tm, PLE_DIM), lambda i, f: (i, 0)),
                  pl.BlockSpec((D_MODEL, tf), lambda i, f: (0, f)),
                  pl.BlockSpec((D_MODEL, tf), lambda i, f: (0, f)),
                  pl.BlockSpec((tf, D_MODEL), lambda i, f: (f, 0)),
                  pl.BlockSpec((D_MODEL, D_MODEL), lambda i, f: (0, 0)),
                  pl.BlockSpec((PLE_DIM, D_MODEL), lambda i, f: (0, 0)),
                  vec, vec],
        out_specs=pl.BlockSpec((tm, D_MODEL), lambda i, f: (i, 0)),
        out_shape=_sds((m, D_MODEL), F32),
        scratch_shapes=[pltpu.VMEM((tm, D_MODEL), BF16),
                        pltpu.VMEM((tm, D_MODEL), F32)],
        compiler_params=_cparams(("arbitrary", "arbitrary")),
        name="ffn_dense",
    )(x, p, w1_b, w3_b, w2_b, pg_b, pp_b, g, b)


def _moe_capacity(tm):
    cap = tm / 4 + 2.3 * math.sqrt(tm * 3 / 16)
    return min(tm, -(-int(math.ceil(cap)) // 16) * 16)


def _moe_body(xb_ref, comb_ref, w1_ref, w3_ref, w2_ref, o_ref, pos_scr, post_scr, combt_scr, xs_scr, ys_scr,
              *, cap):
    e = pl.program_id(1)
    f = pl.program_id(2)
    tm = xb_ref.shape[0]

    @pl.when(jnp.logical_and(e == 0, f == 0))
    def _():
        comb = comb_ref[...]
        sel = jnp.where(comb > 0.0, 1.0, 0.0).astype(BF16)
        r = lax.broadcasted_iota(jnp.int32, (tm, tm), 0)
        c = lax.broadcasted_iota(jnp.int32, (tm, tm), 1)
        pos_scr[...] = jnp.dot(jnp.where(c < r, 1.0, 0.0).astype(BF16), sel, preferred_element_type=F32)
        post_scr[...] = lax.dot_general(sel, jnp.where(r < c, 1.0, 0.0).astype(BF16), (((0,), (0,)), ((), ())),
                                        preferred_element_type=F32)
        combt_scr[...] = comb.T
        o_ref[...] = jnp.zeros(o_ref.shape, F32)

    lane = lax.broadcasted_iota(jnp.int32, (tm, LANES), 1)
    gate = jnp.sum(jnp.where(lane == e, comb_ref[...], 0.0), axis=1, keepdims=True)
    pos = jnp.sum(jnp.where(lane == e, pos_scr[...], 0.0), axis=1, keepdims=True)
    gate_t = combt_scr[pl.ds(e, 1), :]
    pos_t = post_scr[pl.ds(e, 1), :]
    n_rows = jnp.sum(jnp.where(gate_t > 0.0, 1.0, 0.0)).astype(jnp.int32)
    n_chunks = lax.div(n_rows + (cap - 1), cap)

    def chunk(ci, carry):
        base = (ci * cap).astype(F32)
        slot_col = lax.broadcasted_iota(jnp.int32, (1, cap), 1).astype(F32) + base
        slot_row = lax.broadcasted_iota(jnp.int32, (cap, 1), 0).astype(F32) + base
        scatter = jnp.where(jnp.logical_and(gate > 0.0, pos == slot_col), 1.0, 0.0).astype(BF16)

        @pl.when(f == 0)
        def _():
            gather = jnp.where(jnp.logical_and(gate_t > 0.0, pos_t == slot_row), 1.0, 0.0).astype(BF16)
            xs_scr[ci] = jnp.dot(gather, xb_ref[...], preferred_element_type=F32).astype(BF16)
            ys_scr[ci] = jnp.zeros(ys_scr.shape[1:], F32)

        xs = xs_scr[ci]
        h1 = jnp.dot(xs, w1_ref[...], preferred_element_type=F32)
        h3 = jnp.dot(xs, w3_ref[...], preferred_element_type=F32)
        hh = (jax.nn.silu(h1) * h3).astype(BF16)
        ys_scr[ci] += jnp.dot(hh, w2_ref[...], preferred_element_type=F32)

        @pl.when(f == pl.num_programs(2) - 1)
        def _():
            o_ref[...] += gate * jnp.dot(scatter, ys_scr[ci].astype(BF16), preferred_element_type=F32)

        return carry

    lax.fori_loop(0, n_chunks, chunk, 0)


def _moe(xb, comb, w1_b, w3_b, w2_b, tm, tf):
    m = xb.shape[0]
    n_e, _, d_ff = w1_b.shape
    cap = _moe_capacity(tm)
    n_slots = -(-tm // cap)
    return pl.pallas_call(
        functools.partial(_moe_body, cap=cap),
        grid=(m // tm, n_e, d_ff // tf),
        in_specs=[pl.BlockSpec((tm, D_MODEL), lambda i, e, f: (i, 0)),
                  pl.BlockSpec((tm, LANES), lambda i, e, f: (i, 0)),
                  pl.BlockSpec((None, D_MODEL, tf), lambda i, e, f: (e, 0, f)),
                  pl.BlockSpec((None, D_MODEL, tf), lambda i, e, f: (e, 0, f)),
                  pl.BlockSpec((None, tf, D_MODEL), lambda i, e, f: (e, f, 0))],
        out_specs=pl.BlockSpec((tm, D_MODEL), lambda i, e, f: (i, 0)),
        out_shape=_sds((m, D_MODEL), F32),
        scratch_shapes=[pltpu.VMEM((tm, LANES), F32),
                        pltpu.VMEM((LANES, tm), F32),
                        pltpu.VMEM((LANES, tm), F32),
                        pltpu.VMEM((n_slots, cap, D_MODEL), BF16),
                        pltpu.VMEM((n_slots, cap, D_MODEL), F32)],
        compiler_params=_cparams(("arbitrary", "arbitrary", "arbitrary")),
        name="moe",
    )(xb, comb, w1_b, w3_b, w2_b)


def _post_body(x_ref, f_ref, xb_ref, p_ref, pg_ref, pp_ref, g_ref, b_ref, o_ref, *, alpha):
    o_ref[...] = _ple_ln(x_ref[...], f_ref[...], p_ref, pg_ref, pp_ref, g_ref, b_ref, xb_ref[...], alpha)


def _post(x, ffn, xb, p, pg_b, pp_b, g, b, tm, alpha):
    m = x.shape[0]
    rows = pl.BlockSpec((tm, D_MODEL), lambda i: (i, 0))
    vec = pl.BlockSpec((1, D_MODEL), lambda i: (0, 0))
    return pl.pallas_call(
        functools.partial(_post_body, alpha=alpha),
        grid=(m // tm,),
        in_specs=[rows, rows, rows,
                  pl.BlockSpec((tm, PLE_DIM), lambda i: (i, 0)),
                  pl.BlockSpec((D_MODEL, D_MODEL), lambda i: (0, 0)),
                  pl.BlockSpec((PLE_DIM, D_MODEL), lambda i: (0, 0)),
                  vec, vec],
        out_specs=rows,
        out_shape=_sds((m, D_MODEL), F32),
        compiler_params=_cparams(("arbitrary",)),
        name="post",
    )(x, ffn, xb, p, pg_b, pp_b, g, b)


def _row_tile(m, pref):
    return pref if m % pref == 0 else m


def _trunk(x, p, conv_state, h_state, bsz, seq, w, attend):
    depth = w['w_in'].shape[0]
    m = bsz * seq
    alpha = (2.0 * depth) ** 0.25
    tm = _row_tile(m, 512)
    tt = _row_tile(seq, 256)
    xf = x.reshape(m, D_MODEL)
    hs, convs = [], []
    kv_stack = None
    for l in range(depth):
        lam_init = 0.8 - 0.6 * math.exp(-0.3 * l)
        z, zb, *kv_stack = _inproj(xf, w['w_in'][l], _row_tile(m, 2 * tm), l, depth, kv_stack)
        y_rnn, h_last, conv_last = _rnn(z, conv_state[l], h_state[l].reshape(bsz, 1, D_RNN), w['conv_w'][l],
                             w['conv_b'][l].reshape(1, D_RNN), w['lru_wr'][l],
                             w['lru_br'][l].reshape(1, D_RNN), w['lru_wi'][l],
                             w['lru_bi'][l].reshape(1, D_RNN), w['lru_lambda'][l].reshape(1, D_RNN),
                             bsz, seq, tt)
        y_att = attend(l, z, zb, kv_stack, w['lamp'][l], w['subln_g'][l], lam_init)
        mi = l // 2
        moe = l % 2 == 1
        merged = _merge(y_rnn, y_att, z, xf, w['w_rnn_out'][l], w['w_attn_out'][l], w['w_o'][l],
                        w['ln1_g'][l].reshape(1, D_MODEL), w['ln1_b'][l].reshape(1, D_MODEL), tm, alpha,
                        rw=w['router_pad'][mi] if moe else None)
        pl_ = p[l].reshape(m, PLE_DIM)
        ln2 = (w['ln2_g'][l].reshape(1, D_MODEL), w['ln2_b'][l].reshape(1, D_MODEL))
        if moe:
            x1, x1b, comb = merged
            tmoe = _row_tile(m, 2 * tm)
            ffn = _moe(x1b, comb, w['moe_w1'][mi], w['moe_w3'][mi], w['moe_w2'][mi], tmoe,
                       w['moe_w1'].shape[3] // 2)
            xf = _post(x1, ffn, x1b, pl_, w['ple_gate_w'][l], w['ple_proj_w'][l], *ln2, tm, alpha)
        else:
            (x1,) = merged
            xf = _ffn(x1, pl_, w['ffn_w1'][mi], w['ffn_w3'][mi], w['ffn_w2'][mi], w['ple_gate_w'][l],
                      w['ple_proj_w'][l], *ln2, tm, w['ffn_w1'].shape[2] // 2, alpha)
        hs.append(h_last.reshape(bsz, D_RNN))
        convs.append(conv_last)
    k_all, v_all = (t.reshape(depth, bsz, seq, ATT_HEADS, ATT_VDIM) for t in kv_stack)
    return xf.reshape(bsz, seq, D_MODEL), k_all, v_all, jnp.stack(hs), jnp.stack(convs)


def kernel(x_prompt, x_sample, cache_k, cache_v, state_h, state_conv, page_table, p_prompt, p_sample, rel_bias, w_in, conv_w, conv_b, lru_wr, lru_br, lru_wi, lru_bi, lru_lambda, w_rnn_out, w_attn_out, lam_q1, lam_k1, lam_q2, lam_k2, subln_g, w_o, ln1_g, ln1_b, ffn_w1, ffn_w3, ffn_w2, router_w, moe_w1, moe_w3, moe_w2, ple_gate_w, ple_proj_w, ln2_g, ln2_b):
    depth = w_in.shape[0]
    bsz, seq, _ = x_prompt.shape
    dbsz, dseq, _ = x_sample.shape
    n_pool, page_size = cache_k.shape[1], cache_k.shape[2]
    router_f32 = jnp.pad(router_w.astype(F32), ((0, 0), (0, 0), (0, LANES - N_EXPERTS)))
    router_hi = router_f32.astype(BF16)
    router_pad = jnp.stack([router_hi, (router_f32 - router_hi.astype(F32)).astype(BF16)], axis=1)
    w = dict(
        w_in=w_in.astype(BF16), conv_w=conv_w, conv_b=conv_b,
        lru_wr=jax.vmap(_block_diag_groups)(lru_wr).astype(BF16), lru_br=lru_br,
        lru_wi=jax.vmap(_block_diag_groups)(lru_wi).astype(BF16), lru_bi=lru_bi,
        lru_lambda=lru_lambda, w_rnn_out=w_rnn_out.astype(BF16), w_attn_out=w_attn_out.astype(BF16),
        lamp=jnp.stack([lam_q1, lam_k1, lam_q2, lam_k2], axis=1).astype(F32), subln_g=subln_g.astype(F32),
        w_o=w_o.astype(BF16), ln1_g=ln1_g, ln1_b=ln1_b,
        ffn_w1=ffn_w1.astype(BF16), ffn_w3=ffn_w3.astype(BF16), ffn_w2=ffn_w2.astype(BF16),
        router_pad=router_pad, moe_w1=moe_w1.astype(BF16), moe_w3=moe_w3.astype(BF16),
        moe_w2=moe_w2.astype(BF16), ple_gate_w=ple_gate_w.astype(BF16),
        ple_proj_w=ple_proj_w.astype(BF16), ln2_g=ln2_g, ln2_b=ln2_b)

    tq = _row_tile(seq, 256)
    assert tq >= MAX_DIST and page_size >= MAX_DIST
    pbias = _prompt_bias(rel_bias, tq)

    def prompt_attend(l, z, zb, kv_stack, lamp, g, lam_init):
        del l, z, kv_stack
        return _prompt_attend(zb, pbias, lamp, g.reshape(ATT_VDIM, 1), bsz, seq, tq, lam_init)

    conv0 = jnp.zeros((depth, bsz, CONV_W - 1, D_RNN), x_prompt.dtype)
    h0 = jnp.zeros((depth, bsz, D_RNN), F32)
    y_prompt, k_prompt, v_prompt, h_prompt, conv_prompt = _trunk(
        x_prompt, p_prompt, conv0, h0, bsz, seq, w, prompt_attend)

    rows = page_size * ATT_HEADS
    cache_k4 = cache_k.reshape(depth, n_pool, rows, ATT_VDIM)
    cache_v4 = cache_v.reshape(depth, n_pool, rows, ATT_VDIM)
    mask, blast, bnew = _sample_tables(rel_bias, page_size, dseq)
    n_pages = page_table.shape[1]
    n_pp = next(n for n in (16, 8, 4, 2) if n_pages % n == 0)

    def sample_attend(l, z, zb, kv_stack, lamp, g, lam_init):
        del zb
        wq = _sample_wq(z[2], dbsz, dseq)
        knew = kv_stack[0][l].reshape(dbsz, dseq * ATT_HEADS, ATT_VDIM)
        vnew = kv_stack[1][l].reshape(dbsz, dseq * ATT_HEADS, ATT_VDIM)
        o = _sample_attend(l, page_table, cache_k4, cache_v4, wq, knew, vnew, blast, bnew, mask,
                           lamp, g.reshape(ATT_VDIM, 1), n_pp, lam_init)
        o = o.reshape(dbsz, ATT_VDIM, ATT_HEADS, dseq)
        return jnp.transpose(o, (0, 3, 2, 1)).reshape(dbsz * dseq, ATT_HEADS * ATT_VDIM).astype(BF16)

    y_sample, k_sample, v_sample, h_sample, conv_sample = _trunk(
        x_sample, p_sample, state_conv, state_h, dbsz, dseq, w, sample_attend)
    return (y_prompt, y_sample, k_prompt, v_prompt, h_prompt, conv_prompt,
            k_sample, v_sample, h_sample, conv_sample)
```
